```python
import jax, jax.numpy as jnp
from jax import lax
import numpy as np

D_MODEL = 1024
BATCH = 4
SEQ = 4096
DEPTH = 4
DEC_BATCH = 8
DEC_SEQ = 32
PAST_LEN = 4096

CHUNK = 64
DN_HEADS = 4
DN_HEAD_DIM = 128
DN_DIM = DN_HEADS * DN_HEAD_DIM
CONV_WIDTH = 4
CONV_DIM = 3 * DN_DIM
SWA_HEADS = 8
SWA_KV_HEADS = 2
SWA_GROUP = SWA_HEADS // SWA_KV_HEADS
SWA_HEAD_DIM = 64
SWA_DIM = SWA_HEADS * SWA_HEAD_DIM
SWA_KV_DIM = SWA_KV_HEADS * SWA_HEAD_DIM
WINDOW = 128
WINDOW_CHUNKS = WINDOW // CHUNK
D_FF = 2816
RMS_EPS = 1e-6
L2_EPS = 1e-6
NEG_INF = -1e30
IN_SPLITS = (CONV_DIM, DN_DIM, DN_HEADS, DN_HEADS, SWA_DIM, SWA_KV_DIM, SWA_KV_DIM, D_MODEL, D_MODEL)
IN_DIM = sum(IN_SPLITS)
IN_OFFSETS = tuple(int(o) for o in np.cumsum(IN_SPLITS)[:-1])

kernel_name = 'hybrid_gdn_swa_macaron_stream_step'


def rms_norm(x, w):
    xf = x.astype(jnp.float32)
    y = xf * lax.rsqrt(jnp.mean(xf * xf, axis=-1, keepdims=True) + RMS_EPS)
    return (y * w.astype(jnp.float32)).astype(x.dtype)


def l2_norm(x):
    return x * lax.rsqrt(jnp.sum(x * x, axis=-1, keepdims=True) + L2_EPS)


def swiglu(h, wg, wu, wd):
    return (jax.nn.silu(h @ wg) * (h @ wu)) @ wd


def alibi_slopes():
    return 2.0 ** (-8.0 * jnp.arange(1, SWA_HEADS + 1, dtype=jnp.float32) / SWA_HEADS)


def causal_conv(x, buf, w):
    T = x.shape[1]
    xp = jnp.concatenate([buf.astype(x.dtype), x], axis=1)
    y = xp[:, 0:T] * w[0]
    for i in range(1, CONV_WIDTH):
        y = y + xp[:, i:i + T] * w[i]
    return jax.nn.silu(y), xp[:, T:]


def gated_delta_rule(q, k, v, beta, g, S0, chunk):
    B, T, H, DK = q.shape
    DV = v.shape[-1]
    n = T // chunk

    def blocks(a):
        a = a.reshape((B, n, chunk, H) + a.shape[3:])
        return jnp.swapaxes(jnp.moveaxis(a, 1, 0), 2, 3)

    q, k, v, beta, g = (blocks(a) for a in (q, k, v, beta, g))
    g_cum = jnp.cumsum(g, axis=-1)
    idx = jnp.arange(chunk)
    incl = idx[:, None] >= idx[None, :]
    strict = idx[:, None] > idx[None, :]
    decay = jnp.exp(jnp.where(incl, g_cum[..., :, None] - g_cum[..., None, :], -jnp.inf))
    kb = k * beta[..., None]
    lower = jnp.where(strict, jnp.einsum('nbhid,nbhjd->nbhij', kb, k) * decay, 0.0)
    eye = jnp.eye(chunk, dtype=jnp.float32)
    t_mat = lax.linalg.triangular_solve(lower + eye, jnp.broadcast_to(eye, lower.shape),
                                        left_side=True, lower=True)
    w = t_mat @ (kb * jnp.exp(g_cum)[..., None])
    u = t_mat @ (v * beta[..., None])
    q_dec = q * jnp.exp(g_cum)[..., None]
    intra = jnp.einsum('nbhid,nbhjd->nbhij', q, k) * decay
    g_last = g_cum[..., -1]
    k_rem = k * jnp.exp(g_last[..., None] - g_cum)[..., None]

    def step(S, xs):
        w_c, u_c, qd_c, in_c, kr_c, gl_c = xs
        v_new = u_c - w_c @ S
        o = qd_c @ S + in_c @ v_new
        S = S * jnp.exp(gl_c)[..., None, None] + jnp.swapaxes(kr_c, -1, -2) @ v_new
        return S, o

    S, o = lax.scan(step, S0, (w, u, q_dec, intra, k_rem, g_last))
    o = jnp.moveaxis(jnp.swapaxes(o, 2, 3), 0, 1).reshape(B, T, H, DV)
    return o, S


def swa_attention(q, k, v, q_pos, k_pos, sinks):
    s = jnp.einsum('bnqkgd,bnskd->bnkgqs', q.astype(jnp.float32), k.astype(jnp.float32))
    s = s * (SWA_HEAD_DIM ** -0.5)
    slopes = alibi_slopes().reshape(SWA_KV_HEADS, SWA_GROUP)
    dist = jnp.abs(q_pos[:, :, None] - k_pos[:, None, :]).astype(jnp.float32)
    s = s - slopes[None, None, :, :, None, None] * dist[None, :, None, None]
    qc = q_pos[:, :, None] // CHUNK
    kc = k_pos[:, None, :] // CHUNK
    vis = (kc <= qc) & (kc >= qc - WINDOW_CHUNKS) & (k_pos[:, None, :] >= 0)
    s = jnp.where(vis[None, :, None, None], s, NEG_INF)
    sink = sinks.astype(jnp.float32).reshape(SWA_KV_HEADS, SWA_GROUP)[None, None, :, :, None, None]
    m = jnp.maximum(jnp.max(s, axis=-1, keepdims=True), sink)
    p = jnp.exp(s - m)
    p = p / (jnp.sum(p, axis=-1, keepdims=True) + jnp.exp(sink - m))
    return jnp.einsum('bnkgqs,bnskd->bnqkgd', p, v.astype(jnp.float32))


def token_mixer(h, start, conv_buf, S0, k_cache, v_cache, w_in, conv_w, a_log, dt_bias, dn_norm,
                q_norm, k_norm, sinks, w_o_dn, w_o_swa, w_out):
    B, T, _ = h.shape
    z = h @ w_in
    qkv_dn, gate_dn, b_dn, a_dn, q_s, k_s, v_s, g_a, g_b = jnp.split(z, IN_OFFSETS, axis=-1)

    qkv, new_buf = causal_conv(qkv_dn, conv_buf, conv_w)
    qd, kd, vd = jnp.split(qkv.astype(jnp.float32), 3, axis=-1)
    qd = l2_norm(qd.reshape(B, T, DN_HEADS, DN_HEAD_DIM)) * (DN_HEAD_DIM ** -0.5)
    kd = l2_norm(kd.reshape(B, T, DN_HEADS, DN_HEAD_DIM))
    vd = vd.reshape(B, T, DN_HEADS, DN_HEAD_DIM)
    beta = jax.nn.sigmoid(b_dn.astype(jnp.float32))
    g = -jnp.exp(a_log.astype(jnp.float32)) * jax.nn.softplus(a_dn.astype(jnp.float32) + dt_bias.astype(jnp.float32))
    chunk = CHUNK if T >= CHUNK else T
    o_dn, S = gated_delta_rule(qd, kd, vd, beta, g, S0.astype(jnp.float32), chunk)
    o_dn = rms_norm(o_dn, dn_norm) * jax.nn.silu(gate_dn.astype(jnp.float32).reshape(B, T, DN_HEADS, DN_HEAD_DIM))
    y_dn = o_dn.reshape(B, T, DN_DIM).astype(h.dtype) @ w_o_dn

    qs = rms_norm(q_s.reshape(B, T, SWA_KV_HEADS, SWA_GROUP, SWA_HEAD_DIM), q_norm)
    ks = rms_norm(k_s.reshape(B, T, SWA_KV_HEADS, SWA_HEAD_DIM), k_norm)
    vs = v_s.reshape(B, T, SWA_KV_HEADS, SWA_HEAD_DIM)
    if k_cache is None:
        n = T // CHUNK
        band = WINDOW_CHUNKS + 1
        pad = ((0, 0), (WINDOW_CHUNKS * CHUNK, 0), (0, 0), (0, 0))
        kp = jnp.pad(ks, pad).reshape(B, n + WINDOW_CHUNKS, CHUNK, SWA_KV_HEADS, SWA_HEAD_DIM)
        vp = jnp.pad(vs, pad).reshape(B, n + WINDOW_CHUNKS, CHUNK, SWA_KV_HEADS, SWA_HEAD_DIM)
        kb = jnp.stack([kp[:, i:i + n] for i in range(band)], axis=2).reshape(B, n, band * CHUNK, SWA_KV_HEADS, SWA_HEAD_DIM)
        vb = jnp.stack([vp[:, i:i + n] for i in range(band)], axis=2).reshape(B, n, band * CHUNK, SWA_KV_HEADS, SWA_HEAD_DIM)
        q_pos = jnp.arange(T).reshape(n, CHUNK)
        k_pos = (jnp.arange(n)[:, None] - WINDOW_CHUNKS) * CHUNK + jnp.arange(band * CHUNK)[None, :]
        o_s = swa_attention(qs.reshape(B, n, CHUNK, SWA_KV_HEADS, SWA_GROUP, SWA_HEAD_DIM), kb, vb, q_pos, k_pos, sinks)
        keep = min(WINDOW, T)
        k_rows, v_rows = ks[:, T - keep:], vs[:, T - keep:]
    else:
        w_c = k_cache.shape[1]
        q_pos = start + jnp.arange(T)
        k_pos = jnp.concatenate([start - w_c + jnp.arange(w_c), q_pos])[None]
        kc = jnp.concatenate([k_cache.astype(ks.dtype), ks], axis=1)[:, None]
        vc = jnp.concatenate([v_cache.astype(vs.dtype), vs], axis=1)[:, None]
        o_s = swa_attention(qs[:, None], kc, vc, q_pos[None], k_pos, sinks)
        k_rows, v_rows = ks, vs
    y_swa = o_s.reshape(B, T, SWA_DIM).astype(h.dtype) @ w_o_swa

    merged = jax.nn.sigmoid(g_a) * y_dn + jax.nn.sigmoid(g_b) * y_swa
    return merged @ w_out, new_buf, S, k_rows, v_rows


def trunk(x, start, conv_state, dn_state, k_cache, v_cache, params):
    (ffn1_norm, ffn1_wg, ffn1_wu, ffn1_wd, mix_norm, w_in, conv_w, a_log, dt_bias, dn_norm,
     q_norm, k_norm, sinks, w_o_dn, w_o_swa, w_out, ffn2_norm, ffn2_wg, ffn2_wu, ffn2_wd) = params
    B = x.shape[0]
    dn_out, conv_out, k_out, v_out = [], [], [], []
    for l in range(DEPTH):
        x = x + 0.5 * swiglu(rms_norm(x, ffn1_norm[l]), ffn1_wg[l], ffn1_wu[l], ffn1_wd[l])
        if k_cache is None:
            buf = jnp.zeros((B, CONV_WIDTH - 1, CONV_DIM), x.dtype)
            S0 = jnp.zeros((B, DN_HEADS, DN_HEAD_DIM, DN_HEAD_DIM), jnp.float32)
            kc, vc = None, None
        else:
            buf, S0, kc, vc = conv_state[l], dn_state[l], k_cache[l], v_cache[l]
        mix, nb, S, kr, vr = token_mixer(rms_norm(x, mix_norm[l]), start, buf, S0, kc, vc, w_in[l], conv_w[l],
                                         a_log[l], dt_bias[l], dn_norm[l], q_norm[l], k_norm[l], sinks[l],
                                         w_o_dn[l], w_o_swa[l], w_out[l])
        x = x + mix.astype(x.dtype)
        x = x + 0.5 * swiglu(rms_norm(x, ffn2_norm[l]), ffn2_wg[l], ffn2_wu[l], ffn2_wd[l])
        dn_out.append(S)
        conv_out.append(nb)
        k_out.append(kr)
        v_out.append(vr)
    return x, jnp.stack(dn_out), jnp.stack(conv_out), jnp.stack(k_out), jnp.stack(v_out)


def setup_inputs(seed: int = 0) -> dict:
    key = jax.random.key(seed)
    ks = jax.random.split(key, 32)
    f32 = jnp.float32
    nrm = lambda k, shape, scale: jax.random.normal(k, shape, f32) * scale
    gain = lambda k, shape: 1.0 + 0.01 * jax.random.normal(k, shape, f32)
    w_cache = min(WINDOW, PAST_LEN)
    dt = jnp.exp(jax.random.uniform(ks[20], (DEPTH, DN_HEADS), f32, np.log(0.001), np.log(0.1)))
    return {
        'x_prompt': nrm(ks[0], (BATCH, SEQ, D_MODEL), 1.0),
        'x_sample': nrm(ks[1], (DEC_BATCH, DEC_SEQ, D_MODEL), 1.0),
        'state_dn': nrm(ks[2], (DEPTH, DEC_BATCH, DN_HEADS, DN_HEAD_DIM, DN_HEAD_DIM), 0.1),
        'state_conv': nrm(ks[3], (DEPTH, DEC_BATCH, CONV_WIDTH - 1, CONV_DIM), 1.0),
        'cache_swa_k': nrm(ks[4], (DEPTH, DEC_BATCH, w_cache, SWA_KV_HEADS, SWA_HEAD_DIM), 1.0),
        'cache_swa_v': nrm(ks[5], (DEPTH, DEC_BATCH, w_cache, SWA_KV_HEADS, SWA_HEAD_DIM), 1.0),
        'ffn1_norm': gain(ks[6], (DEPTH, D_MODEL)),
        'ffn1_wg': nrm(ks[7], (DEPTH, D_MODEL, D_FF), D_MODEL ** -0.5),
        'ffn1_wu': nrm(ks[8], (DEPTH, D_MODEL, D_FF), D_MODEL ** -0.5),
        'ffn1_wd': nrm(ks[9], (DEPTH, D_FF, D_MODEL), D_FF ** -0.5),
        'mix_norm': gain(ks[10], (DEPTH, D_MODEL)),
        'w_in': nrm(ks[11], (DEPTH, D_MODEL, IN_DIM), D_MODEL ** -0.5),
        'conv_w': nrm(ks[12], (DEPTH, CONV_WIDTH, CONV_DIM), CONV_WIDTH ** -0.5),
        'a_log': jnp.log(jax.random.uniform(ks[13], (DEPTH, DN_HEADS), f32, 1.0, 16.0)),
        'dt_bias': dt + jnp.log(-jnp.expm1(-dt)),
        'dn_norm': gain(ks[14], (DEPTH, DN_HEAD_DIM)),
        'q_norm': gain(ks[15], (DEPTH, SWA_HEAD_DIM)),
        'k_norm': gain(ks[16], (DEPTH, SWA_HEAD_DIM)),
        'sinks': nrm(ks[17], (DEPTH, SWA_HEADS), 0.5),
        'w_o_dn': nrm(ks[18], (DEPTH, DN_DIM, D_MODEL), DN_DIM ** -0.5),
        'w_o_swa': nrm(ks[19], (DEPTH, SWA_DIM, D_MODEL), SWA_DIM ** -0.5),
        'w_out': nrm(ks[21], (DEPTH, D_MODEL, D_MODEL), D_MODEL ** -0.5),
        'ffn2_norm': gain(ks[22], (DEPTH, D_MODEL)),
        'ffn2_wg': nrm(ks[23], (DEPTH, D_MODEL, D_FF), D_MODEL ** -0.5),
        'ffn2_wu': nrm(ks[24], (DEPTH, D_MODEL, D_FF), D_MODEL ** -0.5),
        'ffn2_wd': nrm(ks[25], (DEPTH, D_FF, D_MODEL), D_FF ** -0.5),
    }


def reference(x_prompt, x_sample, state_dn, state_conv, cache_swa_k, cache_swa_v, ffn1_norm, ffn1_wg,
              ffn1_wu, ffn1_wd, mix_norm, w_in, conv_w, a_log, dt_bias, dn_norm, q_norm, k_norm, sinks,
              w_o_dn, w_o_swa, w_out, ffn2_norm, ffn2_wg, ffn2_wu, ffn2_wd):
    params = (ffn1_norm, ffn1_wg, ffn1_wu, ffn1_wd, mix_norm, w_in, conv_w, a_log, dt_bias, dn_norm,
              q_norm, k_norm, sinks, w_o_dn, w_o_swa, w_out, ffn2_norm, ffn2_wg, ffn2_wu, ffn2_wd)
    y_prompt, dn_prompt, conv_prompt, swa_k_prompt, swa_v_prompt = trunk(
        x_prompt, 0, None, None, None, None, params)
    y_sample, dn_sample, conv_sample, swa_k_sample, swa_v_sample = trunk(
        x_sample, PAST_LEN, state_conv, state_dn, cache_swa_k, cache_swa_v, params)
    return (y_prompt, y_sample, dn_prompt, dn_sample, conv_prompt, conv_sample,
            swa_k_prompt, swa_v_prompt, swa_k_sample, swa_v_sample)
```

```python
import functools

import jax
import jax.numpy as jnp
from jax import lax
from jax.experimental import pallas as pl
from jax.experimental.pallas import tpu as pltpu

F32 = jnp.float32
BF16 = jnp.bfloat16

D_MODEL = 1024
D_FF = 2816
DEPTH = 4
CHUNK = 64
DN_HEADS = 4
DN_HEAD_DIM = 128
DN_DIM = DN_HEADS * DN_HEAD_DIM
CONV_WIDTH = 4
CONV_DIM = 3 * DN_DIM
SWA_HEADS = 8
SWA_KV_HEADS = 2
SWA_GROUP = SWA_HEADS // SWA_KV_HEADS
SWA_HEAD_DIM = 64
SWA_DIM = SWA_HEADS * SWA_HEAD_DIM
SWA_KV_DIM = SWA_KV_HEADS * SWA_HEAD_DIM
WINDOW = 128
RMS_EPS = 1e-6
L2_EPS = 1e-6
NEG_INF = -1e30
IN_SPLITS = (CONV_DIM, DN_DIM, DN_HEADS, DN_HEADS, SWA_DIM, SWA_KV_DIM, SWA_KV_DIM, D_MODEL, D_MODEL)

LANES = 128
SOLVE_BLOCK = 16
FF_SPLIT = 1536
VMEM_LIMIT = 56 * 1024 * 1024


def _dot(a, b):
    return jnp.dot(a, b, preferred_element_type=F32)


def _dot_nt(a, b):
    return lax.dot_general(a, b, (((1,), (1,)), ((), ())), preferred_element_type=F32)


def _dot_exact(a, b):
    return jnp.dot(a, b, preferred_element_type=F32, precision=lax.Precision.HIGHEST)


def _rms(x, w):
    return x * lax.rsqrt(jnp.mean(x * x, axis=-1, keepdims=True) + RMS_EPS) * w


def _sigmoid(x):
    return 1.0 / (1.0 + jnp.exp(-x))


def _softplus(x):
    return jnp.maximum(x, 0.0) + jnp.log(1.0 + jnp.exp(-jnp.abs(x)))


def _resident(shape, layer):
    zeros = (0,) * len(shape)
    return pl.BlockSpec((None,) + tuple(shape), lambda *_: (layer,) + zeros,
                        pipeline_mode=pl.Buffered(1))


def _params(n_axes):
    return pltpu.CompilerParams(dimension_semantics=("arbitrary",) * n_axes,
                                vmem_limit_bytes=VMEM_LIMIT)


def _ffn_kernel(x_ref, nw_ref, wg_ref, wu_ref, wd_ref, o_ref):
    x = x_ref[...]
    hb = _rms(x, nw_ref[...]).astype(BF16)
    acc = None
    for lo, hi in ((0, FF_SPLIT), (FF_SPLIT, D_FF)):
        g = _dot(hb, wg_ref[:, lo:hi])
        u = _dot(hb, wu_ref[:, lo:hi])
        a = (g * _sigmoid(g) * u).astype(BF16)
        y = _dot(a, wd_ref[lo:hi, :])
        acc = y if acc is None else acc + y
    o_ref[...] = x + 0.5 * acc


def _ffn(x, nw, wg, wu, wd, layer, tm):
    n = x.shape[0]
    row = pl.BlockSpec((tm, D_MODEL), lambda i: (i, 0))
    return pl.pallas_call(
        _ffn_kernel,
        grid=(n // tm,),
        in_specs=[row, _resident((1, D_MODEL), layer), _resident((D_MODEL, D_FF), layer),
                  _resident((D_MODEL, D_FF), layer), _resident((D_FF, D_MODEL), layer)],
        out_specs=row,
        out_shape=jax.ShapeDtypeStruct((n, D_MODEL), F32),
        compiler_params=_params(1),
        name="ffn",
    )(x, nw, wg, wu, wd)


_IN_GROUPS = (CONV_DIM, DN_DIM, SWA_DIM, SWA_KV_DIM, SWA_KV_DIM, D_MODEL, D_MODEL)
_IN_MAIN = sum(_IN_GROUPS)


def _inproj_kernel(x_ref, nw_ref, w_ref, wba_ref, wbat_ref,
                   qkv_ref, gate_ref, qs_ref, ks_ref, vs_ref, ga_ref, gb_ref, ba_ref, bat_ref):
    hb = _rms(x_ref[...], nw_ref[...]).astype(BF16)
    off = 0
    for ref, width in zip((qkv_ref, gate_ref, qs_ref, ks_ref, vs_ref, ga_ref, gb_ref), _IN_GROUPS):
        ref[...] = _dot(hb, w_ref[:, off:off + width]).astype(ref.dtype)
        off += width
    ba_ref[...] = _dot(hb, wba_ref[...])
    bat_ref[...] = _dot_nt(wbat_ref[...], hb)


def _inproj(x, nw, w_main, w_ba, w_bat, layer, tm):
    n = x.shape[0]
    row = lambda width: pl.BlockSpec((tm, width), lambda i: (i, 0))
    out_shape = [jax.ShapeDtypeStruct((n, w), F32) for w in _IN_GROUPS]
    out_shape += [jax.ShapeDtypeStruct((n, LANES), F32), jax.ShapeDtypeStruct((8, n), F32)]
    out_specs = [row(w) for w in _IN_GROUPS] + [row(LANES), pl.BlockSpec((8, tm), lambda i: (0, i))]
    return pl.pallas_call(
        _inproj_kernel,
        grid=(n // tm,),
        in_specs=[row(D_MODEL), _resident((1, D_MODEL), layer), _resident((D_MODEL, _IN_MAIN), layer),
                  _resident((D_MODEL, LANES), layer), _resident((8, D_MODEL), layer)],
        out_specs=out_specs,
        out_shape=out_shape,
        compiler_params=_params(1),
        name="inproj",
    )(x, nw, w_main, w_ba, w_bat)


def _unit_lower_inverse(a, same_blk, eye):
    mm = lambda x, y: _dot(x.astype(BF16), y.astype(BF16))
    a_diag = jnp.where(same_blk, a, 0.0)
    a_off = a - a_diag
    p = -a_diag
    td = eye + p
    n_sq = SOLVE_BLOCK.bit_length() - 2
    for _ in range(n_sq):
        p = mm(p, p)
        td = td + mm(td, p)
    b = mm(td, a_off)
    b2 = mm(b, b)
    return mm(eye - b, mm(eye + b2, td))


def _dn_kernel(qkv_ref, gate_ref, ba_ref, bat_ref, convw_ref, lanep_ref, rowp_ref, dnw_ref,
               cbuf0_ref, s0_ref,
               o_ref, sout_ref, cout_ref,
               xs_ref, s_ref, w_s, u_s, qd_s, kr_s, in_s, *, tb, chunk):
    i = pl.program_id(1)
    last = pl.num_programs(1) - 1
    nc = tb // chunk

    @pl.when(i == 0)
    def _():
        s_ref[...] = s0_ref[...]
        xs_ref[0:8, :] = jnp.zeros((8, CONV_DIM), F32)
        xs_ref[5:8, :] = cbuf0_ref[...]

    x = qkv_ref[...]
    xs_ref[8:8 + tb, :] = x
    cw = convw_ref[...]
    y = (xs_ref[5:5 + tb, :] * cw[0:1] + xs_ref[6:6 + tb, :] * cw[1:2]
         + xs_ref[7:7 + tb, :] * cw[2:3] + x * cw[3:4])
    tail = xs_ref[8 + tb - 3:8 + tb, :]
    xs_ref[5:8, :] = tail

    @pl.when(i == last)
    def _():
        cout_ref[...] = tail

    y = y * _sigmoid(y)

    ba = ba_ref[...]
    lanep = lanep_ref[...]
    beta_all = _sigmoid(ba)
    g_all = -jnp.exp(lanep[0:1]) * _softplus(ba + lanep[1:2])
    rowp = rowp_ref[...]
    g_rows = -jnp.exp(rowp[0]) * _softplus(bat_ref[...] + rowp[1])

    r = lax.broadcasted_iota(jnp.int32, (tb, tb), 0)
    c = lax.broadcasted_iota(jnp.int32, (tb, tb), 1)
    same_chunk = (r // chunk) == (c // chunk)
    incl = same_chunk & (c <= r)
    strict = same_chunk & (c < r)
    same_blk = (r // SOLVE_BLOCK) == (c // SOLVE_BLOCK)
    eye = jnp.where(r == c, 1.0, 0.0).astype(F32)
    gc_all = _dot_exact(jnp.where(incl, 1.0, 0.0).astype(F32), g_all)
    gc_rows = _dot_exact(g_rows, jnp.where(same_chunk & (r <= c), 1.0, 0.0).astype(F32))

    for h in range(DN_HEADS):
        sl = slice(h * DN_HEAD_DIM, (h + 1) * DN_HEAD_DIM)
        q = y[:, sl]
        k = y[:, DN_DIM + h * DN_HEAD_DIM:DN_DIM + (h + 1) * DN_HEAD_DIM]
        v = y[:, 2 * DN_DIM + h * DN_HEAD_DIM:2 * DN_DIM + (h + 1) * DN_HEAD_DIM]
        q = q * lax.rsqrt(jnp.sum(q * q, axis=-1, keepdims=True) + L2_EPS) * (DN_HEAD_DIM ** -0.5)
        k = k * lax.rsqrt(jnp.sum(k * k, axis=-1, keepdims=True) + L2_EPS)
        beta = beta_all[:, h:h + 1]
        gc = gc_all[:, DN_HEADS + h:DN_HEADS + h + 1]
        gc_row = gc_rows[DN_HEADS + h:DN_HEADS + h + 1, :]
        decay = jnp.exp(jnp.where(incl, gc - gc_row, NEG_INF))
        kb = k * beta
        kbf = k.astype(BF16)
        a = jnp.where(strict, _dot_nt(kb.astype(BF16), kbf) * decay, 0.0)
        intra = _dot_nt(q.astype(BF16), kbf) * decay
        t_mat = _unit_lower_inverse(a, same_blk, eye)
        egc = jnp.exp(gc)
        rhs = jnp.concatenate([kb * egc, v * beta], axis=1).astype(BF16)
        wu = _dot(t_mat.astype(BF16), rhs)
        w_s[h] = wu[:, :DN_HEAD_DIM].astype(BF16)
        u_s[h] = wu[:, DN_HEAD_DIM:]
        qd_s[h] = (q * egc).astype(BF16)
        g_last = jnp.concatenate(
            [jnp.broadcast_to(gc[(j + 1) * chunk - 1:(j + 1) * chunk, :], (chunk, 1)) for j in range(nc)], axis=0)
        kr_s[h] = k * jnp.exp(g_last - gc)
        in_s[h] = intra.astype(BF16)

    dnw = dnw_ref[...]
    for j in range(nc):
        rows = slice(j * chunk, (j + 1) * chunk)
        for h in range(DN_HEADS):
            sl = slice(h * DN_HEAD_DIM, (h + 1) * DN_HEAD_DIM)
            s = s_ref[h]
            sb = s.astype(BF16)
            v_new = u_s[h, rows, :] - _dot(w_s[h, rows, :], sb)
            vb = v_new.astype(BF16)
            o = _dot(qd_s[h, rows, :], sb) + _dot(in_s[h, rows, rows], vb)
            g_end = gc_all[(j + 1) * chunk - 1:(j + 1) * chunk, DN_HEADS + h:DN_HEADS + h + 1]
            s_ref[h] = s * jnp.exp(g_end) + _dot(kr_s[h, rows, :].T.astype(BF16), vb)
            gate = gate_ref[rows, sl]
            o_ref[rows, sl] = (_rms(o, dnw) * (gate * _sigmoid(gate))).astype(o_ref.dtype)

    @pl.when(i == last)
    def _():
        sout_ref[...] = s_ref[...]


def _deltanet(qkv, gate, ba, bat, conv_w, lanep, rowp, dn_w, cbuf0, s0, layer, batch, seq, tb, chunk):
    nb = seq // tb
    row = lambda width: pl.BlockSpec((tb, width), lambda b, i: (b * nb + i, 0))
    per_batch = lambda shape: pl.BlockSpec((None,) + shape, lambda b, i: (b,) + (0,) * len(shape))
    kern = functools.partial(_dn_kernel, tb=tb, chunk=chunk)
    return pl.pallas_call(
        kern,
        grid=(batch, nb),
        in_specs=[row(CONV_DIM), row(DN_DIM), row(LANES),
                  pl.BlockSpec((None, 8, tb), lambda b, i: (b * nb + i, 0, 0)),
                  _resident((CONV_WIDTH, CONV_DIM), layer), _resident((2, LANES), layer),
                  _resident((2, 8, 1), layer), _resident((1, DN_HEAD_DIM), layer),
                  per_batch((CONV_WIDTH - 1, CONV_DIM)), per_batch((DN_HEADS, DN_HEAD_DIM, DN_HEAD_DIM))],
        out_specs=[row(DN_DIM), per_batch((DN_HEADS, DN_HEAD_DIM, DN_HEAD_DIM)),
                   per_batch((CONV_WIDTH - 1, CONV_DIM))],
        out_shape=[jax.ShapeDtypeStruct((batch * seq, DN_DIM), BF16),
                   jax.ShapeDtypeStruct((batch, DN_HEADS, DN_HEAD_DIM, DN_HEAD_DIM), F32),
                   jax.ShapeDtypeStruct((batch, CONV_WIDTH - 1, CONV_DIM), F32)],
        scratch_shapes=[pltpu.VMEM((tb + 8, CONV_DIM), F32),
                        pltpu.VMEM((DN_HEADS, DN_HEAD_DIM, DN_HEAD_DIM), F32),
                        pltpu.VMEM((DN_HEADS, tb, DN_HEAD_DIM), BF16),
                        pltpu.VMEM((DN_HEADS, tb, DN_HEAD_DIM), F32),
                        pltpu.VMEM((DN_HEADS, tb, DN_HEAD_DIM), BF16),
                        pltpu.VMEM((DN_HEADS, tb, DN_HEAD_DIM), F32),
                        pltpu.VMEM((DN_HEADS, tb, tb), BF16)],
        compiler_params=_params(2),
        name="deltanet",
    )(qkv, gate, ba, bat.reshape(8, batch * nb, tb).swapaxes(0, 1), conv_w, lanep, rowp, dn_w, cbuf0, s0)


def _half_rms(x, w, lo):
    x2 = x * x
    s_lo = jnp.sum(jnp.where(lo, x2, 0.0), axis=-1, keepdims=True)
    s_hi = jnp.sum(jnp.where(lo, 0.0, x2), axis=-1, keepdims=True)
    inv = 1.0 / SWA_HEAD_DIM
    scale = jnp.where(lo, lax.rsqrt(s_lo * inv + RMS_EPS), lax.rsqrt(s_hi * inv + RMS_EPS))
    return x * scale * w


def _swa_kernel(sinks_ref, q_ref, k_ref, v_ref, qw_ref, kw_ref, k0_ref, v0_ref,
                o_ref, kn_ref, kbuf, vbuf, *, tq, cq, layer, first_pos):
    i = pl.program_id(1)
    kw_len = WINDOW + cq

    @pl.when(i == 0)
    def _():
        kbuf[0:WINDOW, :] = k0_ref[...]
        vbuf[0:WINDOW, :] = v0_ref[...]

    lo = lax.broadcasted_iota(jnp.int32, (1, LANES), 1) < SWA_HEAD_DIM
    kn = _half_rms(k_ref[...], kw_ref[...], lo)
    kn_ref[...] = kn
    kbuf[WINDOW:WINDOW + tq, :] = kn
    vbuf[WINDOW:WINDOW + tq, :] = v_ref[...]

    def variants(x):
        sw = pltpu.roll(x, SWA_HEAD_DIM, 1)
        a_lo = jnp.where(lo, x, 0.0).astype(BF16)
        a_hi = jnp.where(lo, 0.0, x).astype(BF16)
        b_lo = jnp.where(lo, sw, 0.0).astype(BF16)
        b_hi = jnp.where(lo, 0.0, sw).astype(BF16)
        return {(0, 0): a_lo, (0, 1): b_hi, (1, 0): b_lo, (1, 1): a_hi}

    kvar = variants(kbuf[...])
    vvar = variants(vbuf[...])

    qi = lax.broadcasted_iota(jnp.int32, (cq, kw_len), 0)
    kj = lax.broadcasted_iota(jnp.int32, (cq, kw_len), 1)
    dist = jnp.abs(WINDOW + qi - kj).astype(F32)

    qw = qw_ref[...]
    for t in range(SWA_HEADS // 2):
        qn = (_half_rms(q_ref[:, t * LANES:(t + 1) * LANES], qw[:, t * LANES:(t + 1) * LANES], lo)
              * (SWA_HEAD_DIM ** -0.5)).astype(BF16)
        for j in range(tq // cq):
            rows = slice(j * cq, (j + 1) * cq)
            win = slice(j * cq, j * cq + kw_len)
            acc = None
            for half in range(2):
                h = 2 * t + half
                kv = h // SWA_GROUP
                slope = 2.0 ** (-8.0 * (h + 1) / SWA_HEADS)
                s = _dot_nt(qn[rows, :], kvar[kv, half][win, :]) - slope * dist
                if first_pos < WINDOW:
                    s = jnp.where(kj >= WINDOW - first_pos - i * tq - j * cq, s, NEG_INF)
                sink = sinks_ref[layer, h]
                m = jnp.maximum(jnp.max(s, axis=-1, keepdims=True), sink)
                p = jnp.exp(s - m)
                denom = jnp.sum(p, axis=-1, keepdims=True) + jnp.exp(sink - m)
                o = _dot(p.astype(BF16), vvar[kv, half][win, :]) / denom
                acc = o if acc is None else acc + o
            o_ref[rows, t * LANES:(t + 1) * LANES] = acc.astype(o_ref.dtype)

    if tq >= WINDOW:
        knext = kbuf[tq:tq + WINDOW, :]
        vnext = vbuf[tq:tq + WINDOW, :]
        kbuf[0:WINDOW, :] = knext
        vbuf[0:WINDOW, :] = vnext


def _swa(sinks, q, k, v, qw, kw, k0, v0, layer, batch, seq, tq, cq, first_pos):
    nb = seq // tq
    assert tq >= WINDOW or nb == 1
    row = lambda width: pl.BlockSpec((tq, width), lambda b, i: (b * nb + i, 0))
    per_batch = pl.BlockSpec((None, WINDOW, SWA_KV_DIM), lambda b, i: (b, 0, 0))
    kern = functools.partial(_swa_kernel, tq=tq, cq=cq, layer=layer, first_pos=first_pos)
    return pl.pallas_call(
        kern,
        grid=(batch, nb),
        in_specs=[pl.BlockSpec(memory_space=pltpu.SMEM),
                  row(SWA_DIM), row(SWA_KV_DIM), row(SWA_KV_DIM),
                  _resident((1, SWA_DIM), layer), _resident((1, SWA_KV_DIM), layer),
                  per_batch, per_batch],
        out_specs=[row(SWA_DIM), row(SWA_KV_DIM)],
        out_shape=[jax.ShapeDtypeStruct((batch * seq, SWA_DIM), BF16),
                   jax.ShapeDtypeStruct((batch * seq, SWA_KV_DIM), F32)],
        scratch_shapes=[pltpu.VMEM((WINDOW + tq, SWA_KV_DIM), F32),
                        pltpu.VMEM((WINDOW + tq, SWA_KV_DIM), F32)],
        compiler_params=_params(2),
        name="swa",
    )(sinks, q, k, v, qw, kw, k0, v0)


def _merge_kernel(x_ref, odn_ref, oswa_ref, ga_ref, gb_ref, wdn_ref, wswa_ref, wout_ref, o_ref):
    y_dn = _dot(odn_ref[...], wdn_ref[...])
    y_swa = _dot(oswa_ref[...], wswa_ref[...])
    merged = _sigmoid(ga_ref[...]) * y_dn + _sigmoid(gb_ref[...]) * y_swa
    o_ref[...] = x_ref[...] + _dot(merged.astype(BF16), wout_ref[...])


def _merge(x, odn, oswa, ga, gb, wdn, wswa, wout, layer, tm):
    n = x.shape[0]
    row = lambda width: pl.BlockSpec((tm, width), lambda i: (i, 0))
    return pl.pallas_call(
        _merge_kernel,
        grid=(n // tm,),
        in_specs=[row(D_MODEL), row(DN_DIM), row(SWA_DIM), row(D_MODEL), row(D_MODEL),
                  _resident((DN_DIM, D_MODEL), layer), _resident((SWA_DIM, D_MODEL), layer),
                  _resident((D_MODEL, D_MODEL), layer)],
        out_specs=row(D_MODEL),
        out_shape=jax.ShapeDtypeStruct((n, D_MODEL), F32),
        compiler_params=_params(1),
        name="merge",
    )(x, odn, oswa, ga, gb, wdn, wswa, wout)


def _prepare(ffn1_norm, ffn1_wg, ffn1_wu, ffn1_wd, mix_norm, w_in, conv_w, a_log, dt_bias, dn_norm,
             q_norm, k_norm, sinks, w_o_dn, w_o_swa, w_out, ffn2_norm, ffn2_wg, ffn2_wu, ffn2_wd):
    depth = w_in.shape[0]
    offs = [0]
    for s in IN_SPLITS:
        offs.append(offs[-1] + s)
    col = lambda idx: w_in[:, :, offs[idx]:offs[idx + 1]]
    w_main = jnp.concatenate([col(0), col(1), col(4), col(5), col(6), col(7), col(8)], axis=2).astype(BF16)
    w_ba8 = jnp.concatenate([col(2), col(3)], axis=2)
    w_ba = jnp.pad(w_ba8, ((0, 0), (0, 0), (0, LANES - 2 * DN_HEADS))).astype(BF16)
    w_bat = jnp.swapaxes(w_ba8, 1, 2).astype(BF16)
    zeros4 = jnp.zeros((depth, DN_HEADS), F32)
    lane_pad = lambda a: jnp.pad(jnp.concatenate([zeros4, a.astype(F32)], axis=1),
                                 ((0, 0), (0, LANES - 2 * DN_HEADS)))
    lanep = jnp.stack([lane_pad(a_log), lane_pad(dt_bias)], axis=1)
    row_pad = lambda a: jnp.concatenate([zeros4, a.astype(F32)], axis=1)[:, :, None]
    rowp = jnp.stack([row_pad(a_log), row_pad(dt_bias)], axis=1)
    r3 = lambda a: a.astype(F32)[:, None, :]
    return dict(
        ffn1=(r3(ffn1_norm), ffn1_wg.astype(BF16), ffn1_wu.astype(BF16), ffn1_wd.astype(BF16)),
        ffn2=(r3(ffn2_norm), ffn2_wg.astype(BF16), ffn2_wu.astype(BF16), ffn2_wd.astype(BF16)),
        mix_norm=r3(mix_norm), w_main=w_main, w_ba=w_ba, w_bat=w_bat,
        conv_w=conv_w.astype(F32), lanep=lanep, rowp=rowp, dn_norm=r3(dn_norm),
        q_norm=jnp.tile(q_norm.astype(F32), (1, SWA_HEADS))[:, None, :],
        k_norm=jnp.tile(k_norm.astype(F32), (1, SWA_KV_HEADS))[:, None, :],
        sinks=sinks.astype(F32),
        w_o_dn=w_o_dn.astype(BF16), w_o_swa=w_o_swa.astype(BF16), w_out=w_out.astype(BF16),
    )


def _trunk(x, first_pos, conv_state, dn_state, k_cache, v_cache, p, *, tm, tb, tq):
    batch, seq, _ = x.shape
    depth = p["w_main"].shape[0]
    chunk = CHUNK if seq >= CHUNK else seq
    x = x.reshape(batch * seq, D_MODEL)
    dn_out, conv_out, k_out, v_out = [], [], [], []
    for l in range(depth):
        x = _ffn(x, *p["ffn1"], l, tm)
        qkv, gate, qs, ks, vs, ga, gb, ba, bat = _inproj(x, p["mix_norm"], p["w_main"], p["w_ba"], p["w_bat"], l, tm)
        if conv_state is None:
            cbuf0 = jnp.zeros((batch, CONV_WIDTH - 1, CONV_DIM), F32)
            s0 = jnp.zeros((batch, DN_HEADS, DN_HEAD_DIM, DN_HEAD_DIM), F32)
            k0 = jnp.zeros((batch, WINDOW, SWA_KV_DIM), F32)
            v0 = k0
        else:
            cbuf0, s0 = conv_state[l], dn_state[l]
            k0 = k_cache[l].reshape(batch, WINDOW, SWA_KV_DIM)
            v0 = v_cache[l].reshape(batch, WINDOW, SWA_KV_DIM)
        o_dn, s_new, conv_new = _deltanet(qkv, gate, ba, bat, p["conv_w"], p["lanep"], p["rowp"], p["dn_norm"],
                                          cbuf0, s0, l, batch, seq, tb, chunk)
        o_swa, kn = _swa(p["sinks"], qs, ks, vs, p["q_norm"], p["k_norm"], k0, v0, l, batch, seq, tq, chunk, first_pos)
        x = _merge(x, o_dn, o_swa, ga, gb, p["w_o_dn"], p["w_o_swa"], p["w_out"], l, tm)
        x = _ffn(x, *p["ffn2"], l, tm)
        keep = min(WINDOW, seq)
        dn_out.append(s_new)
        conv_out.append(conv_new)
        k_out.append(kn.reshape(batch, seq, SWA_KV_HEADS, SWA_HEAD_DIM)[:, seq - keep:])
        v_out.append(vs.reshape(batch, seq, SWA_KV_HEADS, SWA_HEAD_DIM)[:, seq - keep:])
    return (x.reshape(batch, seq, D_MODEL), jnp.stack(dn_out), jnp.stack(conv_out),
            jnp.stack(k_out), jnp.stack(v_out))


def kernel(x_prompt, x_sample, state_dn, state_conv, cache_swa_k, cache_swa_v, ffn1_norm, ffn1_wg, ffn1_wu, ffn1_wd, mix_norm, w_in, conv_w, a_log, dt_bias, dn_norm, q_norm, k_norm, sinks, w_o_dn, w_o_swa, w_out, ffn2_norm, ffn2_wg, ffn2_wu, ffn2_wd):
    p = _prepare(ffn1_norm, ffn1_wg, ffn1_wu, ffn1_wd, mix_norm, w_in, conv_w, a_log, dt_bias, dn_norm,
                 q_norm, k_norm, sinks, w_o_dn, w_o_swa, w_out, ffn2_norm, ffn2_wg, ffn2_wu, ffn2_wd)
    past_len = 4096
    assert cache_swa_k.shape[2] == WINDOW and past_len % CHUNK == 0
    y_p, dn_p, conv_p, k_p, v_p = _trunk(x_prompt, 0, None, None, None, None, p, tm=512, tb=256, tq=256)
    dec_rows = x_sample.shape[0] * x_sample.shape[1]
    y_s, dn_s, conv_s, k_s, v_s = _trunk(x_sample, past_len, state_conv, state_dn, cache_swa_k, cache_swa_v, p,
                                         tm=dec_rows, tb=x_sample.shape[1], tq=x_sample.shape[1])
    return (y_p, y_s, dn_p, dn_s, conv_p, conv_s, k_p, v_p, k_s, v_s)
```

```python
import functools

import jax
import jax.numpy as jnp
from jax import lax
from jax.experimental import pallas as pl
from jax.experimental.pallas import tpu as pltpu

F32 = jnp.float32
BF16 = jnp.bfloat16

D_MODEL = 1024
D_FF = 2816
DEPTH = 4
CHUNK = 64
DN_HEADS = 4
DN_HEAD_DIM = 128
DN_DIM = DN_HEADS * DN_HEAD_DIM
CONV_WIDTH = 4
CONV_DIM = 3 * DN_DIM
SWA_HEADS = 8
SWA_KV_HEADS = 2
SWA_GROUP = SWA_HEADS // SWA_KV_HEADS
SWA_HEAD_DIM = 64
SWA_DIM = SWA_HEADS * SWA_HEAD_DIM
SWA_KV_DIM = SWA_KV_HEADS * SWA_HEAD_DIM
WINDOW = 128
WINDOW_CHUNKS = WINDOW // CHUNK
RMS_EPS = 1e-6
L2_EPS = 1e-6
NEG_INF = -1e30
IN_SPLITS = (CONV_DIM, DN_DIM, DN_HEADS, DN_HEADS, SWA_DIM, SWA_KV_DIM, SWA_KV_DIM, D_MODEL, D_MODEL)

LANES = 128
SUBLANES = 8
SOLVE_BLOCK = 16
FF_SPLIT = 1536
VMEM_LIMIT = 56 * 1024 * 1024


def _dot(a, b):
    return jnp.dot(a, b, preferred_element_type=F32)


def _dot_nt(a, b):
    return lax.dot_general(a, b, (((1,), (1,)), ((), ())), preferred_element_type=F32)


def _dot_exact(a, b):
    return jnp.dot(a, b, preferred_element_type=F32, precision=lax.Precision.HIGHEST)


def _rms(x, w):
    return x * lax.rsqrt(jnp.mean(x * x, axis=-1, keepdims=True) + RMS_EPS) * w


def _sigmoid(x):
    return 1.0 / (1.0 + jnp.exp(-x))


def _silu_tanh(x):
    hx = 0.5 * x
    return hx + hx * jnp.tanh(hx)


def _softplus(x):
    return jnp.maximum(x, 0.0) + jnp.log(1.0 + jnp.exp(-jnp.abs(x)))


def _resident(shape, layer):
    zeros = (0,) * len(shape)
    return pl.BlockSpec((None,) + tuple(shape), lambda *_: (layer,) + zeros,
                        pipeline_mode=pl.Buffered(1))


def _params(n_axes):
    return pltpu.CompilerParams(dimension_semantics=("arbitrary",) * n_axes,
                                vmem_limit_bytes=VMEM_LIMIT)


def _ffn_kernel(x_ref, nw_ref, wg_ref, wu_ref, wd_ref, o_ref):
    x = x_ref[...]
    hb = _rms(x, nw_ref[...]).astype(BF16)
    acc = None
    for lo, hi in ((0, FF_SPLIT), (FF_SPLIT, D_FF)):
        g = _dot(hb, wg_ref[:, lo:hi])
        u = _dot(hb, wu_ref[:, lo:hi])
        a = (g * _sigmoid(g) * u).astype(BF16)
        y = _dot(a, wd_ref[lo:hi, :])
        acc = y if acc is None else acc + y
    o_ref[...] = x + 0.5 * acc


def _ffn(x, nw, wg, wu, wd, layer, tm):
    n = x.shape[0]
    row = pl.BlockSpec((tm, D_MODEL), lambda i: (i, 0))
    return pl.pallas_call(
        _ffn_kernel,
        grid=(n // tm,),
        in_specs=[row, _resident((1, D_MODEL), layer), _resident((D_MODEL, D_FF), layer),
                  _resident((D_MODEL, D_FF), layer), _resident((D_FF, D_MODEL), layer)],
        out_specs=row,
        out_shape=jax.ShapeDtypeStruct((n, D_MODEL), F32),
        compiler_params=_params(1),
        name="ffn",
    )(x, nw, wg, wu, wd)


_IN_GROUPS = (CONV_DIM, DN_DIM, SWA_DIM, SWA_KV_DIM, SWA_KV_DIM, D_MODEL, D_MODEL)
_IN_MAIN = sum(_IN_GROUPS)


def _inproj_kernel(x_ref, nw_ref, w_ref, wba_ref, wbat_ref,
                   qkv_ref, gate_ref, qs_ref, ks_ref, vs_ref, ga_ref, gb_ref, ba_ref, bat_ref):
    hb = _rms(x_ref[...], nw_ref[...]).astype(BF16)
    off = 0
    for ref, width in zip((qkv_ref, gate_ref, qs_ref, ks_ref, vs_ref, ga_ref, gb_ref), _IN_GROUPS):
        ref[...] = _dot(hb, w_ref[:, off:off + width]).astype(ref.dtype)
        off += width
    ba_ref[...] = _dot(hb, wba_ref[...])
    bat_ref[...] = _dot_nt(wbat_ref[...], hb)


def _inproj(x, nw, w_main, w_ba, w_bat, layer, tm):
    n = x.shape[0]
    row = lambda width: pl.BlockSpec((tm, width), lambda i: (i, 0))
    out_shape = [jax.ShapeDtypeStruct((n, w), F32) for w in _IN_GROUPS]
    out_shape += [jax.ShapeDtypeStruct((n, LANES), F32), jax.ShapeDtypeStruct((8, n), F32)]
    out_specs = [row(w) for w in _IN_GROUPS] + [row(LANES), pl.BlockSpec((8, tm), lambda i: (0, i))]
    return pl.pallas_call(
        _inproj_kernel,
        grid=(n // tm,),
        in_specs=[row(D_MODEL), _resident((1, D_MODEL), layer), _resident((D_MODEL, _IN_MAIN), layer),
                  _resident((D_MODEL, LANES), layer), _resident((8, D_MODEL), layer)],
        out_specs=out_specs,
        out_shape=out_shape,
        compiler_params=_params(1),
        name="inproj",
    )(x, nw, w_main, w_ba, w_bat)


def _unit_lower_inverses(a_diags, a_offs, eye):
    bf = lambda x: x.astype(BF16)
    ps = [bf(a) for a in a_diags]
    tds = [eye - a for a in a_diags]
    n_sq = SOLVE_BLOCK.bit_length() - 2
    for _ in range(n_sq):
        ps = [bf(_dot(p, p)) for p in ps]
        tds = [td + _dot(bf(td), p) for td, p in zip(tds, ps)]
    tdbs = [bf(td) for td in tds]
    bs = [_dot(tdb, bf(a)) for tdb, a in zip(tdbs, a_offs)]
    bbs = [bf(b) for b in bs]
    b2s = [_dot(bb, bb) for bb in bbs]
    xs = [bf(_dot(bf(eye + b2), tdb)) for b2, tdb in zip(b2s, tdbs)]
    return [_dot(bf(eye - b), x) for b, x in zip(bs, xs)]


def _dn_kernel(qkv_ref, gate_ref, ba_ref, bat_ref, convw_ref, lanep_ref, rowp_ref, dnw_ref,
               cbuf0_ref, s0_ref,
               o_ref, sout_ref, cout_ref,
               c8_ref, s_ref, wq_s, u_s, krt_s, in_s, oraw_s, *, tb, sb, chunk):
    i = pl.program_id(1)
    last = pl.num_programs(1) - 1
    nc = tb // chunk
    cps = sb // chunk

    @pl.when(i == 0)
    def _():
        s_ref[...] = s0_ref[...]
        c8_ref[...] = jnp.zeros((SUBLANES, CONV_DIM), F32)
        c8_ref[SUBLANES - (CONV_WIDTH - 1):SUBLANES, :] = cbuf0_ref[...]

    x = qkv_ref[...]
    cw = convw_ref[...]
    prev = c8_ref[...]
    sub = lax.broadcasted_iota(jnp.int32, (SUBLANES, CONV_DIM), 0)
    y = x * cw[CONV_WIDTH - 1:CONV_WIDTH]
    for sh in range(1, CONV_WIDTH):
        rolled = pltpu.roll(x, sh, 0)
        head = jnp.where(sub < sh, pltpu.roll(prev, sh, 0), rolled[0:SUBLANES])
        shifted = jnp.concatenate([head, rolled[SUBLANES:]], axis=0)
        y = y + shifted * cw[CONV_WIDTH - 1 - sh:CONV_WIDTH - sh]
    c8_ref[...] = x[tb - SUBLANES:tb]

    @pl.when(i == last)
    def _():
        cout_ref[...] = x[tb - (CONV_WIDTH - 1):tb]

    y = _silu_tanh(y)

    ba = ba_ref[...]
    lanep = lanep_ref[...]
    beta_all = _sigmoid(ba)
    g_all = -jnp.exp(lanep[0:1]) * _softplus(ba + lanep[1:2])
    rowp = rowp_ref[...]
    g_rows = -jnp.exp(rowp[0]) * _softplus(bat_ref[...] + rowp[1])

    rt = lax.broadcasted_iota(jnp.int32, (tb, tb), 0)
    ct = lax.broadcasted_iota(jnp.int32, (tb, tb), 1)
    same_chunk_t = (rt // chunk) == (ct // chunk)
    gc_all = _dot_exact(jnp.where(same_chunk_t & (ct <= rt), 1.0, 0.0).astype(F32), g_all)
    gc_rows = _dot_exact(g_rows, jnp.where(same_chunk_t & (rt <= ct), 1.0, 0.0).astype(F32))

    r = lax.broadcasted_iota(jnp.int32, (sb, sb), 0)
    c = lax.broadcasted_iota(jnp.int32, (sb, sb), 1)
    same_chunk = (r // chunk) == (c // chunk)
    incl = same_chunk & (c <= r)
    same_blk = (r // SOLVE_BLOCK) == (c // SOLVE_BLOCK)
    strict_diag = same_blk & (c < r)
    strict_off = same_chunk & (c < r) & jnp.logical_not(same_blk)
    eye = jnp.where(r == c, 1.0, 0.0).astype(F32)

    probs = [(h, s) for h in range(DN_HEADS) for s in range(tb // sb)]
    q_n, k_n, kb_n, gc_n, rhs_n, qd_n = [], [], [], [], [], []
    for h in range(DN_HEADS):
        qh = y[:, h * DN_HEAD_DIM:(h + 1) * DN_HEAD_DIM]
        kh = y[:, DN_DIM + h * DN_HEAD_DIM:DN_DIM + (h + 1) * DN_HEAD_DIM]
        vh = y[:, 2 * DN_DIM + h * DN_HEAD_DIM:2 * DN_DIM + (h + 1) * DN_HEAD_DIM]
        qh = qh * (lax.rsqrt(jnp.sum(qh * qh, axis=-1, keepdims=True) + L2_EPS) * (DN_HEAD_DIM ** -0.5))
        kh = kh * lax.rsqrt(jnp.sum(kh * kh, axis=-1, keepdims=True) + L2_EPS)
        beta_h = beta_all[:, h:h + 1]
        gc_h = gc_all[:, DN_HEADS + h:DN_HEADS + h + 1]
        egc_h = jnp.exp(gc_h)
        kb_h = kh * beta_h
        q_n.append(qh)
        k_n.append(kh)
        kb_n.append(kb_h)
        gc_n.append(gc_h)
        rhs_n.append(jnp.concatenate([kb_h * egc_h, vh * beta_h], axis=1).astype(BF16))
        qd_n.append(qh * egc_h)
    rs = lambda s: slice(s * sb, (s + 1) * sb)
    kts = [k_n[h][rs(s)].T.astype(BF16) for h, s in probs]
    decays = [jnp.exp(jnp.where(incl, gc_n[h][rs(s)] - gc_rows[DN_HEADS + h:DN_HEADS + h + 1, rs(s)], NEG_INF))
              for h, s in probs]
    akds = [_dot(kb_n[h][rs(s)].astype(BF16), kt) * d for (h, s), kt, d in zip(probs, kts, decays)]
    intras = [_dot(q_n[h][rs(s)].astype(BF16), kt) * d for (h, s), kt, d in zip(probs, kts, decays)]
    t_mats = _unit_lower_inverses([jnp.where(strict_diag, a, 0.0) for a in akds],
                                  [jnp.where(strict_off, a, 0.0) for a in akds], eye)
    wus = [_dot(t.astype(BF16), rhs_n[h][rs(s)]) for (h, s), t in zip(probs, t_mats)]
    for (h, s), wu, intra in zip(probs, wus, intras):
        k = k_n[h][rs(s)]
        gc = gc_n[h][rs(s)]
        qd = qd_n[h][rs(s)]
        for cc in range(cps):
            j = s * cps + cc
            rr = slice(cc * chunk, (cc + 1) * chunk)
            wq_s[h, j] = jnp.concatenate([wu[rr, :DN_HEAD_DIM], qd[rr]], axis=0).astype(BF16)
            u_s[h, j] = wu[rr, DN_HEAD_DIM:]
            g_last = gc[(cc + 1) * chunk - 1:(cc + 1) * chunk, :]
            krt_s[h, j] = (k[rr] * jnp.exp(g_last - gc[rr])).T.astype(BF16)
            in_s[h, j] = intra[rr, rr].astype(BF16)

    heads = range(DN_HEADS)
    for j in range(nc):
        rows = slice(j * chunk, (j + 1) * chunk)
        s_old = [s_ref[h] for h in heads]
        wqs = [_dot(wq_s[h, j], s_old[h].astype(BF16)) for h in heads]
        v_new = [(u_s[h, j] - wqs[h][:chunk]).astype(BF16) for h in heads]
        for h in heads:
            g_end = gc_all[(j + 1) * chunk - 1:(j + 1) * chunk, DN_HEADS + h:DN_HEADS + h + 1]
            s_ref[h] = s_old[h] * jnp.exp(g_end) + _dot(krt_s[h, j], v_new[h])
        for h in heads:
            oraw_s[rows, h * DN_HEAD_DIM:(h + 1) * DN_HEAD_DIM] = wqs[h][chunk:] + _dot(in_s[h, j], v_new[h])

    dnw = dnw_ref[...]
    for h in range(DN_HEADS):
        sl = slice(h * DN_HEAD_DIM, (h + 1) * DN_HEAD_DIM)
        o_ref[:, sl] = (_rms(oraw_s[:, sl], dnw) * _silu_tanh(gate_ref[:, sl])).astype(o_ref.dtype)

    @pl.when(i == last)
    def _():
        sout_ref[...] = s_ref[...]


def _deltanet(qkv, gate, ba, bat, conv_w, lanep, rowp, dn_w, cbuf0, s0, layer, batch, seq, tb, sb, chunk):
    nb = seq // tb
    nc = tb // chunk
    row = lambda width: pl.BlockSpec((tb, width), lambda b, i: (b * nb + i, 0))
    per_batch = lambda shape: pl.BlockSpec((None,) + shape, lambda b, i: (b,) + (0,) * len(shape))
    kern = functools.partial(_dn_kernel, tb=tb, sb=sb, chunk=chunk)
    return pl.pallas_call(
        kern,
        grid=(batch, nb),
        in_specs=[row(CONV_DIM), row(DN_DIM), row(LANES),
                  pl.BlockSpec((None, 8, tb), lambda b, i: (b * nb + i, 0, 0)),
                  _resident((CONV_WIDTH, CONV_DIM), layer), _resident((2, LANES), layer),
                  _resident((2, 8, 1), layer), _resident((1, DN_HEAD_DIM), layer),
                  per_batch((CONV_WIDTH - 1, CONV_DIM)), per_batch((DN_HEADS, DN_HEAD_DIM, DN_HEAD_DIM))],
        out_specs=[row(DN_DIM), per_batch((DN_HEADS, DN_HEAD_DIM, DN_HEAD_DIM)),
                   per_batch((CONV_WIDTH - 1, CONV_DIM))],
        out_shape=[jax.ShapeDtypeStruct((batch * seq, DN_DIM), BF16),
                   jax.ShapeDtypeStruct((batch, DN_HEADS, DN_HEAD_DIM, DN_HEAD_DIM), F32),
                   jax.ShapeDtypeStruct((batch, CONV_WIDTH - 1, CONV_DIM), F32)],
        scratch_shapes=[pltpu.VMEM((SUBLANES, CONV_DIM), F32),
                        pltpu.VMEM((DN_HEADS, DN_HEAD_DIM, DN_HEAD_DIM), F32),
                        pltpu.VMEM((DN_HEADS, nc, 2 * chunk, DN_HEAD_DIM), BF16),
                        pltpu.VMEM((DN_HEADS, nc, chunk, DN_HEAD_DIM), F32),
                        pltpu.VMEM((DN_HEADS, nc, DN_HEAD_DIM, chunk), BF16),
                        pltpu.VMEM((DN_HEADS, nc, chunk, chunk), BF16),
                        pltpu.VMEM((tb, DN_DIM), F32)],
        compiler_params=_params(2),
        name="deltanet",
    )(qkv, gate, ba, bat.reshape(8, batch * nb, tb).swapaxes(0, 1), conv_w, lanep, rowp, dn_w, cbuf0, s0)


def _half_rms(x, w, lo):
    x2 = x * x
    s_lo = jnp.sum(jnp.where(lo, x2, 0.0), axis=-1, keepdims=True)
    s_hi = jnp.sum(jnp.where(lo, 0.0, x2), axis=-1, keepdims=True)
    inv = 1.0 / SWA_HEAD_DIM
    scale = jnp.where(lo, lax.rsqrt(s_lo * inv + RMS_EPS), lax.rsqrt(s_hi * inv + RMS_EPS))
    return x * scale * w


def _swa_kernel(sinks_ref, q_ref, k_ref, v_ref, qw_ref, kw_ref, k0_ref, v0_ref,
                o_ref, kn_ref, kbuf, vbuf, bias_s, *, tq, gq, layer, first_pos):
    i = pl.program_id(1)
    kw_len = WINDOW + gq
    combos = [(kv, half) for kv in range(SWA_KV_HEADS) for half in range(2)]

    @pl.when(i == 0)
    def _():
        kbuf[0:WINDOW, :] = k0_ref[...]
        vbuf[0:WINDOW, :] = v0_ref[...]

    qi = lax.broadcasted_iota(jnp.int32, (2 * gq, kw_len), 0)
    kj = lax.broadcasted_iota(jnp.int32, (2 * gq, kw_len), 1)
    upper = qi >= gq

    @pl.when((pl.program_id(0) == 0) & (i == 0))
    def _():
        ri = jnp.where(upper, qi - gq, qi)
        dist = jnp.abs(WINDOW + ri - kj).astype(F32)
        qc = (ri + WINDOW) // CHUNK
        kc = kj // CHUNK
        vis = (kc <= qc) & (kc >= qc - WINDOW_CHUNKS)
        for n, (kv, half) in enumerate(combos):
            h_a = SWA_GROUP * kv + half
            slope = jnp.where(upper, 2.0 ** (-8.0 * (h_a + 3) / SWA_HEADS), 2.0 ** (-8.0 * (h_a + 1) / SWA_HEADS))
            bias_s[n] = jnp.where(vis, -slope * dist, NEG_INF)

    lo = lax.broadcasted_iota(jnp.int32, (1, LANES), 1) < SWA_HEAD_DIM
    kn = _half_rms(k_ref[...], kw_ref[...], lo)
    kn_ref[...] = kn
    kbuf[WINDOW:WINDOW + tq, :] = kn
    vbuf[WINDOW:WINDOW + tq, :] = v_ref[...]

    def variants(x):
        sw = pltpu.roll(x, SWA_HEAD_DIM, 1)
        a_lo = jnp.where(lo, x, 0.0).astype(BF16)
        a_hi = jnp.where(lo, 0.0, x).astype(BF16)
        b_lo = jnp.where(lo, sw, 0.0).astype(BF16)
        b_hi = jnp.where(lo, 0.0, sw).astype(BF16)
        return {(0, 0): a_lo, (0, 1): b_hi, (1, 0): b_lo, (1, 1): a_hi}

    kvar = variants(kbuf[...])
    vvar = variants(vbuf[...])

    qw = qw_ref[...]
    qn = [(_half_rms(q_ref[:, t * LANES:(t + 1) * LANES], qw[:, t * LANES:(t + 1) * LANES], lo)
           * (SWA_HEAD_DIM ** -0.5)).astype(BF16) for t in range(SWA_HEADS // 2)]

    for g in range(tq // gq):
        rows = slice(g * gq, (g + 1) * gq)
        win = slice(g * gq, g * gq + kw_len)
        for kv in range(SWA_KV_HEADS):
            q2 = jnp.concatenate([qn[2 * kv][rows], qn[2 * kv + 1][rows]], axis=0)
            acc = None
            for half in range(2):
                n = 2 * kv + half
                h_a = SWA_GROUP * kv + half
                s = _dot_nt(q2, kvar[kv, half][win]) + bias_s[n]
                if first_pos < WINDOW and g * gq < WINDOW:
                    s = jnp.where(kj >= WINDOW - first_pos - i * tq - g * gq, s, NEG_INF)
                sink = jnp.where(upper[:, 0:1], sinks_ref[layer, h_a + 2], sinks_ref[layer, h_a])
                m = jnp.maximum(jnp.max(s, axis=-1, keepdims=True), sink)
                p = jnp.exp(s - m)
                denom = jnp.sum(p, axis=-1, keepdims=True) + jnp.exp(sink - m)
                o = _dot(p.astype(BF16), vvar[kv, half][win]) / denom
                acc = o if acc is None else acc + o
            o_ref[rows, 2 * kv * LANES:(2 * kv + 1) * LANES] = acc[:gq].astype(o_ref.dtype)
            o_ref[rows, (2 * kv + 1) * LANES:(2 * kv + 2) * LANES] = acc[gq:].astype(o_ref.dtype)

    if tq >= WINDOW:
        knext = kbuf[tq:tq + WINDOW, :]
        vnext = vbuf[tq:tq + WINDOW, :]
        kbuf[0:WINDOW, :] = knext
        vbuf[0:WINDOW, :] = vnext


def _swa(sinks, q, k, v, qw, kw, k0, v0, layer, batch, seq, tq, gq, first_pos):
    nb = seq // tq
    assert tq >= WINDOW or nb == 1
    assert gq % CHUNK == 0 or (gq == seq and nb == 1)
    assert first_pos % CHUNK == 0
    row = lambda width: pl.BlockSpec((tq, width), lambda b, i: (b * nb + i, 0))
    per_batch = pl.BlockSpec((None, WINDOW, SWA_KV_DIM), lambda b, i: (b, 0, 0))
    kern = functools.partial(_swa_kernel, tq=tq, gq=gq, layer=layer, first_pos=first_pos)
    return pl.pallas_call(
        kern,
        grid=(batch, nb),
        in_specs=[pl.BlockSpec(memory_space=pltpu.SMEM),
                  row(SWA_DIM), row(SWA_KV_DIM), row(SWA_KV_DIM),
                  _resident((1, SWA_DIM), layer), _resident((1, SWA_KV_DIM), layer),
                  per_batch, per_batch],
        out_specs=[row(SWA_DIM), row(SWA_KV_DIM)],
        out_shape=[jax.ShapeDtypeStruct((batch * seq, SWA_DIM), BF16),
                   jax.ShapeDtypeStruct((batch * seq, SWA_KV_DIM), F32)],
        scratch_shapes=[pltpu.VMEM((WINDOW + tq, SWA_KV_DIM), F32),
                        pltpu.VMEM((WINDOW + tq, SWA_KV_DIM), F32),
                        pltpu.VMEM((2 * SWA_KV_HEADS, 2 * gq, WINDOW + gq), F32)],
        compiler_params=_params(2),
        name="swa",
    )(sinks, q, k, v, qw, kw, k0, v0)


def _merge_kernel(x_ref, odn_ref, oswa_ref, ga_ref, gb_ref, wdn_ref, wswa_ref, wout_ref, o_ref):
    y_dn = _dot(odn_ref[...], wdn_ref[...])
    y_swa = _dot(oswa_ref[...], wswa_ref[...])
    merged = _sigmoid(ga_ref[...]) * y_dn + _sigmoid(gb_ref[...]) * y_swa
    o_ref[...] = x_ref[...] + _dot(merged.astype(BF16), wout_ref[...])


def _merge(x, odn, oswa, ga, gb, wdn, wswa, wout, layer, tm):
    n = x.shape[0]
    row = lambda width: pl.BlockSpec((tm, width), lambda i: (i, 0))
    return pl.pallas_call(
        _merge_kernel,
        grid=(n // tm,),
        in_specs=[row(D_MODEL), row(DN_DIM), row(SWA_DIM), row(D_MODEL), row(D_MODEL),
                  _resident((DN_DIM, D_MODEL), layer), _resident((SWA_DIM, D_MODEL), layer),
                  _resident((D_MODEL, D_MODEL), layer)],
        out_specs=row(D_MODEL),
        out_shape=jax.ShapeDtypeStruct((n, D_MODEL), F32),
        compiler_params=_params(1),
        name="merge",
    )(x, odn, oswa, ga, gb, wdn, wswa, wout)


def _prepare(ffn1_norm, ffn1_wg, ffn1_wu, ffn1_wd, mix_norm, w_in, conv_w, a_log, dt_bias, dn_norm,
             q_norm, k_norm, sinks, w_o_dn, w_o_swa, w_out, ffn2_norm, ffn2_wg, ffn2_wu, ffn2_wd):
    depth = w_in.shape[0]
    offs = [0]
    for s in IN_SPLITS:
        offs.append(offs[-1] + s)
    col = lambda idx: w_in[:, :, offs[idx]:offs[idx + 1]]
    w_main = jnp.concatenate([col(0), col(1), col(4), col(5), col(6), col(7), col(8)], axis=2).astype(BF16)
    w_ba8 = jnp.concatenate([col(2), col(3)], axis=2)
    w_ba = jnp.pad(w_ba8, ((0, 0), (0, 0), (0, LANES - 2 * DN_HEADS))).astype(BF16)
    w_bat = jnp.swapaxes(w_ba8, 1, 2).astype(BF16)
    zeros4 = jnp.zeros((depth, DN_HEADS), F32)
    lane_pad = lambda a: jnp.pad(jnp.concatenate([zeros4, a.astype(F32)], axis=1),
                                 ((0, 0), (0, LANES - 2 * DN_HEADS)))
    lanep = jnp.stack([lane_pad(a_log), lane_pad(dt_bias)], axis=1)
    row_pad = lambda a: jnp.concatenate([zeros4, a.astype(F32)], axis=1)[:, :, None]
    rowp = jnp.stack([row_pad(a_log), row_pad(dt_bias)], axis=1)
    r3 = lambda a: a.astype(F32)[:, None, :]
    return dict(
        ffn1=(r3(ffn1_norm), ffn1_wg.astype(BF16), ffn1_wu.astype(BF16), ffn1_wd.astype(BF16)),
        ffn2=(r3(ffn2_norm), ffn2_wg.astype(BF16), ffn2_wu.astype(BF16), ffn2_wd.astype(BF16)),
        mix_norm=r3(mix_norm), w_main=w_main, w_ba=w_ba, w_bat=w_bat,
        conv_w=conv_w.astype(F32), lanep=lanep, rowp=rowp, dn_norm=r3(dn_norm),
        q_norm=jnp.tile(q_norm.astype(F32), (1, SWA_HEADS))[:, None, :],
        k_norm=jnp.tile(k_norm.astype(F32), (1, SWA_KV_HEADS))[:, None, :],
        sinks=sinks.astype(F32),
        w_o_dn=w_o_dn.astype(BF16), w_o_swa=w_o_swa.astype(BF16), w_out=w_out.astype(BF16),
    )


def _trunk(x, first_pos, conv_state, dn_state, k_cache, v_cache, p, *, tm, tb, sb, tq, gq):
    batch, seq, _ = x.shape
    depth = p["w_main"].shape[0]
    chunk = CHUNK if seq >= CHUNK else seq
    x = x.reshape(batch * seq, D_MODEL)
    dn_out, conv_out, k_out, v_out = [], [], [], []
    for l in range(depth):
        x = _ffn(x, *p["ffn1"], l, tm)
        qkv, gate, qs, ks, vs, ga, gb, ba, bat = _inproj(x, p["mix_norm"], p["w_main"], p["w_ba"], p["w_bat"], l, tm)
        if conv_state is None:
            cbuf0 = jnp.zeros((batch, CONV_WIDTH - 1, CONV_DIM), F32)
            s0 = jnp.zeros((batch, DN_HEADS, DN_HEAD_DIM, DN_HEAD_DIM), F32)
            k0 = jnp.zeros((batch, WINDOW, SWA_KV_DIM), F32)
            v0 = k0
        else:
            cbuf0, s0 = conv_state[l], dn_state[l]
            k0 = k_cache[l].reshape(batch, WINDOW, SWA_KV_DIM)
            v0 = v_cache[l].reshape(batch, WINDOW, SWA_KV_DIM)
        o_dn, s_new, conv_new = _deltanet(qkv, gate, ba, bat, p["conv_w"], p["lanep"], p["rowp"], p["dn_norm"],
                                          cbuf0, s0, l, batch, seq, tb, sb, chunk)
        o_swa, kn = _swa(p["sinks"], qs, ks, vs, p["q_norm"], p["k_norm"], k0, v0, l, batch, seq, tq, gq, first_pos)
        x = _merge(x, o_dn, o_swa, ga, gb, p["w_o_dn"], p["w_o_swa"], p["w_out"], l, tm)
        x = _ffn(x, *p["ffn2"], l, tm)
        keep = min(WINDOW, seq)
        dn_out.append(s_new)
        conv_out.append(conv_new)
        k_out.append(kn.reshape(batch, seq, SWA_KV_HEADS, SWA_HEAD_DIM)[:, seq - keep:])
        v_out.append(vs.reshape(batch, seq, SWA_KV_HEADS, SWA_HEAD_DIM)[:, seq - keep:])
    return (x.reshape(batch, seq, D_MODEL), jnp.stack(dn_out), jnp.stack(conv_out),
            jnp.stack(k_out), jnp.stack(v_out))


def kernel(x_prompt, x_sample, state_dn, state_conv, cache_swa_k, cache_swa_v, ffn1_norm, ffn1_wg, ffn1_wu, ffn1_wd, mix_norm, w_in, conv_w, a_log, dt_bias, dn_norm, q_norm, k_norm, sinks, w_o_dn, w_o_swa, w_out, ffn2_norm, ffn2_wg, ffn2_wu, ffn2_wd):
    p = _prepare(ffn1_norm, ffn1_wg, ffn1_wu, ffn1_wd, mix_norm, w_in, conv_w, a_log, dt_bias, dn_norm,
                 q_norm, k_norm, sinks, w_o_dn, w_o_swa, w_out, ffn2_norm, ffn2_wg, ffn2_wu, ffn2_wd)
    past_len = 4096
    assert cache_swa_k.shape[2] == WINDOW
    y_p, dn_p, conv_p, k_p, v_p = _trunk(x_prompt, 0, None, None, None, None, p,
                                         tm=512, tb=256, sb=128, tq=256, gq=128)
    dec_seq = x_sample.shape[1]
    y_s, dn_s, conv_s, k_s, v_s = _trunk(x_sample, past_len, state_conv, state_dn, cache_swa_k, cache_swa_v, p,
                                         tm=x_sample.shape[0] * dec_seq, tb=dec_seq, sb=dec_seq, tq=dec_seq, gq=dec_seq)
    return (y_p, y_s, dn_p, dn_s, conv_p, conv_s, k_p, v_p, k_s, v_s)
```

```python
import functools

import jax
import jax.numpy as jnp
from jax import lax
from jax.experimental import pallas as pl
from jax.experimental.pallas import tpu as pltpu

F32 = jnp.float32
BF16 = jnp.bfloat16

D_MODEL = 1024
D_FF = 2816
DEPTH = 4
CHUNK = 64
DN_HEADS = 4
DN_HEAD_DIM = 128
DN_DIM = DN_HEADS * DN_HEAD_DIM
CONV_WIDTH = 4
CONV_DIM = 3 * DN_DIM
SWA_HEADS = 8
SWA_KV_HEADS = 2
SWA_GROUP = SWA_HEADS // SWA_KV_HEADS
SWA_HEAD_DIM = 64
SWA_DIM = SWA_HEADS * SWA_HEAD_DIM
SWA_KV_DIM = SWA_KV_HEADS * SWA_HEAD_DIM
WINDOW = 128
WINDOW_CHUNKS = WINDOW // CHUNK
RMS_EPS = 1e-6
L2_EPS = 1e-6
NEG_INF = -1e30
IN_SPLITS = (CONV_DIM, DN_DIM, DN_HEADS, DN_HEADS, SWA_DIM, SWA_KV_DIM, SWA_KV_DIM, D_MODEL, D_MODEL)

LANES = 128
SUBLANES = 8
SOLVE_BLOCK = 16
FF_SPLIT = 1536
VMEM_LIMIT = 56 * 1024 * 1024


def _dot(a, b):
    return jnp.dot(a, b, preferred_element_type=F32)


def _dot_nt(a, b):
    return lax.dot_general(a, b, (((1,), (1,)), ((), ())), preferred_element_type=F32)


def _dot_exact(a, b):
    return jnp.dot(a, b, preferred_element_type=F32, precision=lax.Precision.HIGHEST)


def _rms(x, w):
    return x * lax.rsqrt(jnp.mean(x * x, axis=-1, keepdims=True) + RMS_EPS) * w


def _sigmoid(x):
    return 1.0 / (1.0 + jnp.exp(-x))


def _silu_tanh(x):
    hx = 0.5 * x
    return hx + hx * jnp.tanh(hx)


def _softplus(x):
    return jnp.maximum(x, 0.0) + jnp.log(1.0 + jnp.exp(-jnp.abs(x)))


def _resident(shape, layer):
    zeros = (0,) * len(shape)
    return pl.BlockSpec((None,) + tuple(shape), lambda *_: (layer,) + zeros,
                        pipeline_mode=pl.Buffered(1))


def _params(n_axes):
    return pltpu.CompilerParams(dimension_semantics=("arbitrary",) * n_axes,
                                vmem_limit_bytes=VMEM_LIMIT)


def _ffn_kernel(x_ref, nw_ref, wg_ref, wu_ref, wd_ref, o_ref):
    x = x_ref[...]
    hb = _rms(x, nw_ref[...]).astype(BF16)
    acc = None
    for lo, hi in ((0, FF_SPLIT), (FF_SPLIT, D_FF)):
        g = _dot(hb, wg_ref[:, lo:hi])
        u = _dot(hb, wu_ref[:, lo:hi])
        a = (g * _sigmoid(g) * u).astype(BF16)
        y = _dot(a, wd_ref[lo:hi, :])
        acc = y if acc is None else acc + y
    o_ref[...] = x + 0.5 * acc


def _ffn(x, nw, wg, wu, wd, layer, tm):
    n = x.shape[0]
    row = pl.BlockSpec((tm, D_MODEL), lambda i: (i, 0))
    return pl.pallas_call(
        _ffn_kernel,
        grid=(n // tm,),
        in_specs=[row, _resident((1, D_MODEL), layer), _resident((D_MODEL, D_FF), layer),
                  _resident((D_MODEL, D_FF), layer), _resident((D_FF, D_MODEL), layer)],
        out_specs=row,
        out_shape=jax.ShapeDtypeStruct((n, D_MODEL), F32),
        compiler_params=_params(1),
        name="ffn",
    )(x, nw, wg, wu, wd)


_IN_GROUPS = (CONV_DIM, DN_DIM, SWA_DIM, SWA_KV_DIM, SWA_KV_DIM, D_MODEL, D_MODEL)
_IN_MAIN = sum(_IN_GROUPS)
_IN_PIECE = 512


def _conv_silu(x, prev, cw):
    sub = lax.broadcasted_iota(jnp.int32, (SUBLANES, x.shape[1]), 0)
    y = x * cw[CONV_WIDTH - 1:CONV_WIDTH]
    for sh in range(1, CONV_WIDTH):
        rolled = pltpu.roll(x, sh, 0)
        head = jnp.where(sub < sh, pltpu.roll(prev, sh, 0), rolled[0:SUBLANES])
        shifted = jnp.concatenate([head, rolled[SUBLANES:]], axis=0)
        y = y + shifted * cw[CONV_WIDTH - 1 - sh:CONV_WIDTH - sh]
    return _silu_tanh(y)


def _l2_heads(y):
    q_n, k_n, v_n = [], [], []
    for h in range(DN_HEADS):
        q = y[:, h * DN_HEAD_DIM:(h + 1) * DN_HEAD_DIM]
        k = y[:, DN_DIM + h * DN_HEAD_DIM:DN_DIM + (h + 1) * DN_HEAD_DIM]
        q_n.append(q * (lax.rsqrt(jnp.sum(q * q, axis=-1, keepdims=True) + L2_EPS) * (DN_HEAD_DIM ** -0.5)))
        k_n.append(k * lax.rsqrt(jnp.sum(k * k, axis=-1, keepdims=True) + L2_EPS))
        v_n.append(y[:, 2 * DN_DIM + h * DN_HEAD_DIM:2 * DN_DIM + (h + 1) * DN_HEAD_DIM])
    return q_n, k_n, v_n


def _inproj_kernel(x_ref, nw_ref, w_ref, wba_ref, wbat_ref, *rest, tm, tiles_per_seq):
    fused = tiles_per_seq > 0
    if fused:
        (convw_ref, cbuf0_ref, qn_ref, kn_ref, vv_ref, ctail_ref,
         gate_ref, qs_ref, ks_ref, vs_ref, ga_ref, gb_ref, ba_ref, bat_ref, c8_ref) = rest
    else:
        qkv_ref, gate_ref, qs_ref, ks_ref, vs_ref, ga_ref, gb_ref, ba_ref, bat_ref = rest
    if fused:
        @pl.when(pl.program_id(0) == 0)
        def _():
            c8_ref[...] = jnp.zeros((SUBLANES, CONV_DIM), F32)

    hb = _rms(x_ref[...], nw_ref[...]).astype(BF16)
    z = _dot(hb, w_ref[:, 0:CONV_DIM])
    if fused:
        init8 = jnp.concatenate([jnp.zeros((SUBLANES - (CONV_WIDTH - 1), CONV_DIM), F32), cbuf0_ref[...]], axis=0)
        prev8 = jnp.where(pl.program_id(0) % tiles_per_seq == 0, init8, c8_ref[...])
        c8_ref[...] = z[tm - SUBLANES:tm]
        ctail_ref[...] = z[tm - (CONV_WIDTH - 1):tm]
        cw = convw_ref[...]
    else:
        qkv_ref[...] = z
    pieces = []
    off = CONV_DIM
    for ref, width in zip((gate_ref, qs_ref, ks_ref, vs_ref, ga_ref, gb_ref), _IN_GROUPS[1:]):
        for lo in range(0, width, _IN_PIECE):
            pieces.append((width, ref, lo, min(lo + _IN_PIECE, width), off))
        off += width
    pieces.sort(key=lambda t: -t[0])
    slab = tm // len(pieces) if tm % (len(pieces) * 2 * SUBLANES) == 0 else tm
    for g, (_, ref, lo, hi, off) in enumerate(pieces):
        ref[:, lo:hi] = _dot(hb, w_ref[:, off + lo:off + hi]).astype(ref.dtype)
        if fused and g * slab < tm:
            rows = slice(g * slab, (g + 1) * slab)
            hist = prev8 if g == 0 else z[g * slab - SUBLANES:g * slab]
            q_n, k_n, v_n = _l2_heads(_conv_silu(z[rows], hist, cw))
            for h in range(DN_HEADS):
                sl = slice(h * DN_HEAD_DIM, (h + 1) * DN_HEAD_DIM)
                qn_ref[rows, sl] = q_n[h].astype(BF16)
                kn_ref[rows, sl] = k_n[h].astype(BF16)
                vv_ref[rows, sl] = v_n[h].astype(BF16)
    ba_ref[...] = _dot(hb, wba_ref[...])
    bat_ref[...] = _dot_nt(wbat_ref[...], hb)


def _inproj(x, nw, w_main, w_ba, w_bat, conv_w, cbuf0, layer, tm, seq, fused):
    n = x.shape[0]
    row = lambda width: pl.BlockSpec((tm, width), lambda i: (i, 0))
    tail_shapes = [jax.ShapeDtypeStruct((n, w), dt)
                   for w, dt in zip(_IN_GROUPS[1:], (BF16, BF16, F32, F32, BF16, BF16))]
    tail_shapes += [jax.ShapeDtypeStruct((n, LANES), F32), jax.ShapeDtypeStruct((8, n), F32)]
    tail_specs = [row(w) for w in _IN_GROUPS[1:]] + [row(LANES), pl.BlockSpec((8, tm), lambda i: (0, i))]
    in_specs = [row(D_MODEL), _resident((1, D_MODEL), layer), _resident((D_MODEL, _IN_MAIN), layer),
                _resident((D_MODEL, LANES), layer), _resident((8, D_MODEL), layer)]
    args = [x, nw, w_main, w_ba, w_bat]
    if fused:
        assert seq % tm == 0
        tps = seq // tm
        per_seq = pl.BlockSpec((None, CONV_WIDTH - 1, CONV_DIM), lambda i: (i // tps, 0, 0))
        in_specs += [_resident((CONV_WIDTH, CONV_DIM), layer), per_seq]
        args += [conv_w, cbuf0]
        head_shapes = [jax.ShapeDtypeStruct((n, DN_DIM), BF16)] * 3
        head_shapes += [jax.ShapeDtypeStruct((n // seq, CONV_WIDTH - 1, CONV_DIM), F32)]
        head_specs = [row(DN_DIM)] * 3 + [per_seq]
        scratch = [pltpu.VMEM((SUBLANES, CONV_DIM), F32)]
    else:
        tps = 0
        head_shapes = [jax.ShapeDtypeStruct((n, CONV_DIM), F32)]
        head_specs = [row(CONV_DIM)]
        scratch = []
    outs = pl.pallas_call(
        functools.partial(_inproj_kernel, tm=tm, tiles_per_seq=tps),
        grid=(n // tm,),
        in_specs=in_specs,
        out_specs=head_specs + tail_specs,
        out_shape=head_shapes + tail_shapes,
        scratch_shapes=scratch,
        compiler_params=_params(1),
        name="inproj",
    )(*args)
    nh = len(head_shapes)
    return (tuple(outs[:nh]),) + tuple(outs[nh:])


def _unit_lower_inverses(a_diags, a_offs, eye):
    bf = lambda x: x.astype(BF16)
    n = eye.shape[0]
    ps = [bf(a) for a in a_diags]
    tds = [eye - a for a in a_diags]
    ps = [bf(_dot(p, p)) for p in ps]
    n_sq = SOLVE_BLOCK.bit_length() - 2
    for _ in range(n_sq - 1):
        both = [_dot(jnp.concatenate([bf(td), p], axis=0), p) for td, p in zip(tds, ps)]
        tds = [td + r[:n] for td, r in zip(tds, both)]
        ps = [bf(r[n:]) for r in both]
    tds = [td + _dot(bf(td), p) for td, p in zip(tds, ps)]
    tdbs = [bf(td) for td in tds]
    bs = [_dot(tdb, bf(a)) for tdb, a in zip(tdbs, a_offs)]
    bbs = [bf(b) for b in bs]
    b2s = [_dot(bb, bb) for bb in bbs]
    xs = [bf(_dot(bf(eye + b2), tdb)) for b2, tdb in zip(b2s, tdbs)]
    return [_dot(bf(eye - b), x) for b, x in zip(bs, xs)]


def _dn_kernel(*refs, tb, sb, chunk, pre):
    if pre:
        (qn_ref, kn_ref, vv_ref, gate_ref, ba_ref, bat_ref, lanep_ref, rowp_ref, dnw_ref, s0_ref,
         o_ref, sout_ref, s_ref) = refs
    else:
        (qkv_ref, gate_ref, ba_ref, bat_ref, lanep_ref, rowp_ref, dnw_ref, s0_ref, convw_ref, cbuf0_ref,
         o_ref, sout_ref, cout_ref, s_ref, c8_ref) = refs
    i = pl.program_id(1)
    last = pl.num_programs(1) - 1
    nc = tb // chunk
    cps = sb // chunk

    @pl.when(i == 0)
    def _():
        s_ref[...] = s0_ref[...]

    if pre:
        hsl = lambda h: slice(h * DN_HEAD_DIM, (h + 1) * DN_HEAD_DIM)
        q_n = [qn_ref[:, hsl(h)].astype(F32) for h in range(DN_HEADS)]
        k_n = [kn_ref[:, hsl(h)].astype(F32) for h in range(DN_HEADS)]
        v_n = [vv_ref[:, hsl(h)].astype(F32) for h in range(DN_HEADS)]
    else:
        @pl.when(i == 0)
        def _():
            c8_ref[...] = jnp.zeros((SUBLANES, CONV_DIM), F32)
            c8_ref[SUBLANES - (CONV_WIDTH - 1):SUBLANES, :] = cbuf0_ref[...]

        x = qkv_ref[...]
        q_n, k_n, v_n = _l2_heads(_conv_silu(x, c8_ref[...], convw_ref[...]))
        c8_ref[...] = x[tb - SUBLANES:tb]

        @pl.when(i == last)
        def _():
            cout_ref[...] = x[tb - (CONV_WIDTH - 1):tb]

    ba = ba_ref[...]
    lanep = lanep_ref[...]
    beta_all = _sigmoid(ba)
    g_all = -jnp.exp(lanep[0:1]) * _softplus(ba + lanep[1:2])
    rowp = rowp_ref[...]
    g_rows = -jnp.exp(rowp[0]) * _softplus(bat_ref[...] + rowp[1])

    r = lax.broadcasted_iota(jnp.int32, (sb, sb), 0)
    c = lax.broadcasted_iota(jnp.int32, (sb, sb), 1)
    same_chunk = (r // chunk) == (c // chunk)
    incl = same_chunk & (c <= r)
    def split3(a, axis):
        rnd = lambda t: t.astype(BF16).astype(F32)
        hi = rnd(a)
        mid = rnd(a - hi)
        lo = rnd(a - hi - mid)
        return jnp.concatenate([hi, mid, lo], axis=axis).astype(BF16)

    tri_lower = jnp.where(incl, 1.0, 0.0).astype(BF16)
    tri_upper = jnp.where(same_chunk & (r <= c), 1.0, 0.0).astype(BF16)
    g_all3 = split3(g_all, 1)
    g_rows3 = split3(g_rows, 0)
    col_parts, row_parts = [], []
    for s in range(tb // sb):
        cs = _dot(tri_lower, g_all3[s * sb:(s + 1) * sb])
        col_parts.append(cs[:, :LANES] + cs[:, LANES:2 * LANES] + cs[:, 2 * LANES:])
        rsum = _dot(g_rows3[:, s * sb:(s + 1) * sb], tri_upper)
        row_parts.append(rsum[0:8] + rsum[8:16] + rsum[16:24])
    gc_all = jnp.concatenate(col_parts, axis=0) if len(col_parts) > 1 else col_parts[0]
    gc_rows = jnp.concatenate(row_parts, axis=1) if len(row_parts) > 1 else row_parts[0]

    same_blk = (r // SOLVE_BLOCK) == (c // SOLVE_BLOCK)
    strict_diag = same_blk & (c < r)
    strict_off = same_chunk & (c < r) & jnp.logical_not(same_blk)
    eye = jnp.where(r == c, 1.0, 0.0).astype(F32)

    probs = [(h, s) for h in range(DN_HEADS) for s in range(tb // sb)]
    kb_n, gc_n, rhs_n, qd_n = [], [], [], []
    for h in range(DN_HEADS):
        beta_h = beta_all[:, h:h + 1]
        gc_h = gc_all[:, DN_HEADS + h:DN_HEADS + h + 1]
        egc_h = jnp.exp(gc_h)
        kb_h = k_n[h] * beta_h
        kb_n.append(kb_h)
        gc_n.append(gc_h)
        rhs_n.append(jnp.concatenate([kb_h * egc_h, v_n[h] * beta_h], axis=1).astype(BF16))
        qd_n.append(q_n[h] * egc_h)
    rs = lambda s: slice(s * sb, (s + 1) * sb)
    kts = [k_n[h][rs(s)].T.astype(BF16) for h, s in probs]
    decays = [jnp.exp(jnp.where(incl, gc_n[h][rs(s)] - gc_rows[DN_HEADS + h:DN_HEADS + h + 1, rs(s)], NEG_INF))
              for h, s in probs]
    kq = [_dot(jnp.concatenate([kb_n[h][rs(s)], q_n[h][rs(s)]], axis=0).astype(BF16), kt)
          for (h, s), kt in zip(probs, kts)]
    akds = [r[:sb] * d for r, d in zip(kq, decays)]
    intras = [r[sb:] * d for r, d in zip(kq, decays)]
    t_mats = _unit_lower_inverses([jnp.where(strict_diag, a, 0.0) for a in akds],
                                  [jnp.where(strict_off, a, 0.0) for a in akds], eye)
    wus = [_dot(t.astype(BF16), rhs_n[h][rs(s)]) for (h, s), t in zip(probs, t_mats)]
    keys, lhs, wub, qds = [], [], [], []
    for (h, s), wu, intra in zip(probs, wus, intras):
        k = k_n[h][rs(s)]
        gc = gc_n[h][rs(s)]
        for cc in range(cps):
            rr = slice(cc * chunk, (cc + 1) * chunk)
            g_last = gc[(cc + 1) * chunk - 1:(cc + 1) * chunk, :]
            krt = (k[rr] * jnp.exp(g_last - gc[rr])).T
            keys.append((h, s * cps + cc))
            lhs.append(jnp.concatenate([krt, intra[rr, rr]], axis=0).astype(BF16))
            wub.append(wu[rr].astype(BF16))
            qds.append(qd_n[h][rs(s)][rr])
    pre = [_dot(a, b) for a, b in zip(lhs, wub)]
    hd = DN_HEAD_DIM
    gq_v = {key: jnp.concatenate([r[:hd, :hd], qd - r[hd:, :hd]], axis=0).astype(BF16)
            for key, r, qd in zip(keys, pre, qds)}
    nu_v = {key: r[:hd, hd:] for key, r in zip(keys, pre)}
    o0_v = {key: r[hd:, hd:] for key, r in zip(keys, pre)}

    heads = range(DN_HEADS)
    state = [s_ref[h] for h in heads]
    o_rows = [[] for _ in heads]
    for j in range(nc):
        res = [_dot(gq_v[h, j], state[h].astype(BF16)) for h in heads]
        for h in heads:
            g_end = gc_all[(j + 1) * chunk - 1:(j + 1) * chunk, DN_HEADS + h:DN_HEADS + h + 1]
            state[h] = state[h] * jnp.exp(g_end) - res[h][:hd] + nu_v[h, j]
        for h in heads:
            o_rows[h].append(res[h][hd:] + o0_v[h, j])

    dnw = dnw_ref[...]
    for h in heads:
        sl = slice(h * DN_HEAD_DIM, (h + 1) * DN_HEAD_DIM)
        s_ref[h] = state[h]
        o_h = jnp.concatenate(o_rows[h], axis=0) if nc > 1 else o_rows[h][0]
        o_ref[:, sl] = (_rms(o_h, dnw) * _silu_tanh(gate_ref[:, sl].astype(F32))).astype(o_ref.dtype)

    @pl.when(i == last)
    def _():
        sout_ref[...] = s_ref[...]


def _deltanet(dn_in, gate, ba, bat, conv_w, lanep, rowp, dn_w, cbuf0, s0, layer, batch, seq, tb, sb, chunk):
    pre = len(dn_in) > 1
    nb = seq // tb
    nc = tb // chunk
    row = lambda width: pl.BlockSpec((tb, width), lambda b, i: (b * nb + i, 0))
    per_batch = lambda shape: pl.BlockSpec((None,) + shape, lambda b, i: (b,) + (0,) * len(shape))
    state = (DN_HEADS, DN_HEAD_DIM, DN_HEAD_DIM)
    common_specs = [row(DN_DIM), row(LANES), pl.BlockSpec((None, 8, tb), lambda b, i: (b * nb + i, 0, 0)),
                    _resident((2, LANES), layer), _resident((2, 8, 1), layer), _resident((1, DN_HEAD_DIM), layer),
                    per_batch(state)]
    common_args = [gate, ba, bat.reshape(8, batch * nb, tb).swapaxes(0, 1), lanep, rowp, dn_w, s0]
    out_specs = [row(DN_DIM), per_batch(state)]
    out_shape = [jax.ShapeDtypeStruct((batch * seq, DN_DIM), BF16), jax.ShapeDtypeStruct((batch,) + state, F32)]
    scratch = [pltpu.VMEM(state, F32)]
    if pre:
        in_specs = [row(DN_DIM)] * 3 + common_specs
        args = list(dn_in[:3]) + common_args
    else:
        conv_rows = (CONV_WIDTH - 1, CONV_DIM)
        in_specs = [row(CONV_DIM)] + common_specs + [_resident((CONV_WIDTH, CONV_DIM), layer), per_batch(conv_rows)]
        args = [dn_in[0]] + common_args + [conv_w, cbuf0]
        out_specs.append(per_batch(conv_rows))
        out_shape.append(jax.ShapeDtypeStruct((batch,) + conv_rows, F32))
        scratch.append(pltpu.VMEM((SUBLANES, CONV_DIM), F32))
    outs = pl.pallas_call(
        functools.partial(_dn_kernel, tb=tb, sb=sb, chunk=chunk, pre=pre),
        grid=(batch, nb),
        in_specs=in_specs,
        out_specs=out_specs,
        out_shape=out_shape,
        scratch_shapes=scratch,
        compiler_params=_params(2),
        name="deltanet",
    )(*args)
    return outs[0], outs[1], (dn_in[3] if pre else outs[2])


def _half_rms(x, w, lo):
    x2 = x * x
    s_lo = jnp.sum(jnp.where(lo, x2, 0.0), axis=-1, keepdims=True)
    s_hi = jnp.sum(jnp.where(lo, 0.0, x2), axis=-1, keepdims=True)
    inv = 1.0 / SWA_HEAD_DIM
    scale = jnp.where(lo, lax.rsqrt(s_lo * inv + RMS_EPS), lax.rsqrt(s_hi * inv + RMS_EPS))
    return x * scale * w


def _swa_kernel(sinks_ref, q_ref, k_ref, v_ref, qw_ref, kw_ref, k0_ref, v0_ref,
                o_ref, kn_ref, kbuf, vbuf, bias_s, *, tq, gq, layer, first_pos):
    i = pl.program_id(1)
    kw_len = WINDOW + gq
    combos = [(kv, half) for kv in range(SWA_KV_HEADS) for half in range(2)]

    @pl.when(i == 0)
    def _():
        kbuf[0:WINDOW, :] = k0_ref[...]
        vbuf[0:WINDOW, :] = v0_ref[...]

    qi = lax.broadcasted_iota(jnp.int32, (2 * gq, kw_len), 0)
    kj = lax.broadcasted_iota(jnp.int32, (2 * gq, kw_len), 1)
    upper = qi >= gq

    @pl.when((pl.program_id(0) == 0) & (i == 0))
    def _():
        ri = jnp.where(upper, qi - gq, qi)
        dist = jnp.abs(WINDOW + ri - kj).astype(F32)
        qc = (ri + WINDOW) // CHUNK
        kc = kj // CHUNK
        vis = (kc <= qc) & (kc >= qc - WINDOW_CHUNKS)
        for n, (kv, half) in enumerate(combos):
            h_a = SWA_GROUP * kv + half
            slope = jnp.where(upper, 2.0 ** (-8.0 * (h_a + 3) / SWA_HEADS), 2.0 ** (-8.0 * (h_a + 1) / SWA_HEADS))
            bias_s[n] = jnp.where(vis, -slope * dist, NEG_INF)

    lo = lax.broadcasted_iota(jnp.int32, (1, LANES), 1) < SWA_HEAD_DIM
    kn = _half_rms(k_ref[...], kw_ref[...], lo)
    kn_ref[...] = kn
    kbuf[WINDOW:WINDOW + tq, :] = kn
    vbuf[WINDOW:WINDOW + tq, :] = v_ref[...]

    def variants(x):
        sw = pltpu.roll(x, SWA_HEAD_DIM, 1)
        a_lo = jnp.where(lo, x, 0.0).astype(BF16)
        a_hi = jnp.where(lo, 0.0, x).astype(BF16)
        b_lo = jnp.where(lo, sw, 0.0).astype(BF16)
        b_hi = jnp.where(lo, 0.0, sw).astype(BF16)
        return {(0, 0): a_lo, (0, 1): b_hi, (1, 0): b_lo, (1, 1): a_hi}

    kvar = variants(kbuf[...])
    vvar = variants(vbuf[...])

    qw = qw_ref[...]
    qn = [(_half_rms(q_ref[:, t * LANES:(t + 1) * LANES].astype(F32), qw[:, t * LANES:(t + 1) * LANES], lo)
           * (SWA_HEAD_DIM ** -0.5)).astype(BF16) for t in range(SWA_HEADS // 2)]

    for g in range(tq // gq):
        rows = slice(g * gq, (g + 1) * gq)
        win = slice(g * gq, g * gq + kw_len)
        for kv in range(SWA_KV_HEADS):
            q2 = jnp.concatenate([qn[2 * kv][rows], qn[2 * kv + 1][rows]], axis=0)
            acc = None
            for half in range(2):
                n = 2 * kv + half
                h_a = SWA_GROUP * kv + half
                s = _dot_nt(q2, kvar[kv, half][win]) + bias_s[n]
                if first_pos < WINDOW and g * gq < WINDOW:
                    s = jnp.where(kj >= WINDOW - first_pos - i * tq - g * gq, s, NEG_INF)
                sink = jnp.where(upper[:, 0:1], sinks_ref[layer, h_a + 2], sinks_ref[layer, h_a])
                m = jnp.maximum(jnp.max(s, axis=-1, keepdims=True), sink)
                p = jnp.exp(s - m)
                denom = jnp.sum(p, axis=-1, keepdims=True) + jnp.exp(sink - m)
                o = _dot(p.astype(BF16), vvar[kv, half][win]) / denom
                acc = o if acc is None else acc + o
            o_ref[rows, 2 * kv * LANES:(2 * kv + 1) * LANES] = acc[:gq].astype(o_ref.dtype)
            o_ref[rows, (2 * kv + 1) * LANES:(2 * kv + 2) * LANES] = acc[gq:].astype(o_ref.dtype)

    if tq >= WINDOW:
        knext = kbuf[tq:tq + WINDOW, :]
        vnext = vbuf[tq:tq + WINDOW, :]
        kbuf[0:WINDOW, :] = knext
        vbuf[0:WINDOW, :] = vnext


def _swa(sinks, q, k, v, qw, kw, k0, v0, layer, batch, seq, tq, gq, first_pos):
    nb = seq // tq
    assert tq >= WINDOW or nb == 1
    assert gq % CHUNK == 0 or (gq == seq and nb == 1)
    assert first_pos % CHUNK == 0
    row = lambda width: pl.BlockSpec((tq, width), lambda b, i: (b * nb + i, 0))
    per_batch = pl.BlockSpec((None, WINDOW, SWA_KV_DIM), lambda b, i: (b, 0, 0))
    kern = functools.partial(_swa_kernel, tq=tq, gq=gq, layer=layer, first_pos=first_pos)
    return pl.pallas_call(
        kern,
        grid=(batch, nb),
        in_specs=[pl.BlockSpec(memory_space=pltpu.SMEM),
                  row(SWA_DIM), row(SWA_KV_DIM), row(SWA_KV_DIM),
                  _resident((1, SWA_DIM), layer), _resident((1, SWA_KV_DIM), layer),
                  per_batch, per_batch],
        out_specs=[row(SWA_DIM), row(SWA_KV_DIM)],
        out_shape=[jax.ShapeDtypeStruct((batch * seq, SWA_DIM), BF16),
                   jax.ShapeDtypeStruct((batch * seq, SWA_KV_DIM), F32)],
        scratch_shapes=[pltpu.VMEM((WINDOW + tq, SWA_KV_DIM), F32),
                        pltpu.VMEM((WINDOW + tq, SWA_KV_DIM), F32),
                        pltpu.VMEM((2 * SWA_KV_HEADS, 2 * gq, WINDOW + gq), F32)],
        compiler_params=_params(2),
        name="swa",
    )(sinks, q, k, v, qw, kw, k0, v0)


def _merge_kernel(x_ref, odn_ref, oswa_ref, ga_ref, gb_ref, wdn_ref, wswa_ref, wout_ref, o_ref):
    y_dn = _dot(odn_ref[...], wdn_ref[...])
    y_swa = _dot(oswa_ref[...], wswa_ref[...])
    merged = _sigmoid(ga_ref[...].astype(F32)) * y_dn + _sigmoid(gb_ref[...].astype(F32)) * y_swa
    o_ref[...] = x_ref[...] + _dot(merged.astype(BF16), wout_ref[...])


def _merge(x, odn, oswa, ga, gb, wdn, wswa, wout, layer, tm):
    n = x.shape[0]
    row = lambda width: pl.BlockSpec((tm, width), lambda i: (i, 0))
    return pl.pallas_call(
        _merge_kernel,
        grid=(n // tm,),
        in_specs=[row(D_MODEL), row(DN_DIM), row(SWA_DIM), row(D_MODEL), row(D_MODEL),
                  _resident((DN_DIM, D_MODEL), layer), _resident((SWA_DIM, D_MODEL), layer),
                  _resident((D_MODEL, D_MODEL), layer)],
        out_specs=row(D_MODEL),
        out_shape=jax.ShapeDtypeStruct((n, D_MODEL), F32),
        compiler_params=_params(1),
        name="merge",
    )(x, odn, oswa, ga, gb, wdn, wswa, wout)


def _prepare(ffn1_norm, ffn1_wg, ffn1_wu, ffn1_wd, mix_norm, w_in, conv_w, a_log, dt_bias, dn_norm,
             q_norm, k_norm, sinks, w_o_dn, w_o_swa, w_out, ffn2_norm, ffn2_wg, ffn2_wu, ffn2_wd):
    depth = w_in.shape[0]
    offs = [0]
    for s in IN_SPLITS:
        offs.append(offs[-1] + s)
    col = lambda idx: w_in[:, :, offs[idx]:offs[idx + 1]]
    w_main = jnp.concatenate([col(0), col(1), col(4), col(5), col(6), col(7), col(8)], axis=2).astype(BF16)
    w_ba8 = jnp.concatenate([col(2), col(3)], axis=2)
    w_ba = jnp.pad(w_ba8, ((0, 0), (0, 0), (0, LANES - 2 * DN_HEADS))).astype(BF16)
    w_bat = jnp.swapaxes(w_ba8, 1, 2).astype(BF16)
    zeros4 = jnp.zeros((depth, DN_HEADS), F32)
    lane_pad = lambda a: jnp.pad(jnp.concatenate([zeros4, a.astype(F32)], axis=1),
                                 ((0, 0), (0, LANES - 2 * DN_HEADS)))
    lanep = jnp.stack([lane_pad(a_log), lane_pad(dt_bias)], axis=1)
    row_pad = lambda a: jnp.concatenate([zeros4, a.astype(F32)], axis=1)[:, :, None]
    rowp = jnp.stack([row_pad(a_log), row_pad(dt_bias)], axis=1)
    r3 = lambda a: a.astype(F32)[:, None, :]
    return dict(
        ffn1=(r3(ffn1_norm), ffn1_wg.astype(BF16), ffn1_wu.astype(BF16), ffn1_wd.astype(BF16)),
        ffn2=(r3(ffn2_norm), ffn2_wg.astype(BF16), ffn2_wu.astype(BF16), ffn2_wd.astype(BF16)),
        mix_norm=r3(mix_norm), w_main=w_main, w_ba=w_ba, w_bat=w_bat,
        conv_w=conv_w.astype(F32), lanep=lanep, rowp=rowp, dn_norm=r3(dn_norm),
        q_norm=jnp.tile(q_norm.astype(F32), (1, SWA_HEADS))[:, None, :],
        k_norm=jnp.tile(k_norm.astype(F32), (1, SWA_KV_HEADS))[:, None, :],
        sinks=sinks.astype(F32),
        w_o_dn=w_o_dn.astype(BF16), w_o_swa=w_o_swa.astype(BF16), w_out=w_out.astype(BF16),
    )


def _trunk(x, first_pos, conv_state, dn_state, k_cache, v_cache, p, *, tm, tb, sb, tq, gq):
    batch, seq, _ = x.shape
    depth = p["w_main"].shape[0]
    chunk = CHUNK if seq >= CHUNK else seq
    x = x.reshape(batch * seq, D_MODEL)
    dn_out, conv_out, k_out, v_out = [], [], [], []
    for l in range(depth):
        x = _ffn(x, *p["ffn1"], l, tm)
        if conv_state is None:
            cbuf0 = jnp.zeros((batch, CONV_WIDTH - 1, CONV_DIM), F32)
            s0 = jnp.zeros((batch, DN_HEADS, DN_HEAD_DIM, DN_HEAD_DIM), F32)
            k0 = jnp.zeros((batch, WINDOW, SWA_KV_DIM), F32)
            v0 = k0
        else:
            cbuf0, s0 = conv_state[l], dn_state[l]
            k0 = k_cache[l].reshape(batch, WINDOW, SWA_KV_DIM)
            v0 = v_cache[l].reshape(batch, WINDOW, SWA_KV_DIM)
        dn_in, gate, qs, ks, vs, ga, gb, ba, bat = _inproj(
            x, p["mix_norm"], p["w_main"], p["w_ba"], p["w_bat"], p["conv_w"], cbuf0, l, tm, seq, seq % tm == 0)
        o_dn, s_new, conv_new = _deltanet(dn_in, gate, ba, bat, p["conv_w"], p["lanep"], p["rowp"], p["dn_norm"],
                                          cbuf0, s0, l, batch, seq, tb, sb, chunk)
        o_swa, kn = _swa(p["sinks"], qs, ks, vs, p["q_norm"], p["k_norm"], k0, v0, l, batch, seq, tq, gq, first_pos)
        x = _merge(x, o_dn, o_swa, ga, gb, p["w_o_dn"], p["w_o_swa"], p["w_out"], l, tm)
        x = _ffn(x, *p["ffn2"], l, tm)
        keep = min(WINDOW, seq)
        dn_out.append(s_new)
        conv_out.append(conv_new)
        k_out.append(kn.reshape(batch, seq, SWA_KV_HEADS, SWA_HEAD_DIM)[:, seq - keep:])
        v_out.append(vs.reshape(batch, seq, SWA_KV_HEADS, SWA_HEAD_DIM)[:, seq - keep:])
    return (x.reshape(batch, seq, D_MODEL), jnp.stack(dn_out), jnp.stack(conv_out),
            jnp.stack(k_out), jnp.stack(v_out))


def kernel(x_prompt, x_sample, state_dn, state_conv, cache_swa_k, cache_swa_v, ffn1_norm, ffn1_wg, ffn1_wu, ffn1_wd, mix_norm, w_in, conv_w, a_log, dt_bias, dn_norm, q_norm, k_norm, sinks, w_o_dn, w_o_swa, w_out, ffn2_norm, ffn2_wg, ffn2_wu, ffn2_wd):
    p = _prepare(ffn1_norm, ffn1_wg, ffn1_wu, ffn1_wd, mix_norm, w_in, conv_w, a_log, dt_bias, dn_norm,
                 q_norm, k_norm, sinks, w_o_dn, w_o_swa, w_out, ffn2_norm, ffn2_wg, ffn2_wu, ffn2_wd)
    past_len = 4096
    assert cache_swa_k.shape[2] == WINDOW
    y_p, dn_p, conv_p, k_p, v_p = _trunk(x_prompt, 0, None, None, None, None, p,
                                         tm=512, tb=512, sb=128, tq=256, gq=128)
    dec_seq = x_sample.shape[1]
    y_s, dn_s, conv_s, k_s, v_s = _trunk(x_sample, past_len, state_conv, state_dn, cache_swa_k, cache_swa_v, p,
                                         tm=x_sample.shape[0] * dec_seq, tb=dec_seq, sb=dec_seq, tq=dec_seq, gq=dec_seq)
    return (y_p, y_s, dn_p, dn_s, conv_p, conv_s, k_p, v_p, k_s, v_s)
```

```python
import functools

import jax
import jax.numpy as jnp
from jax import lax
from jax.experimental import pallas as pl
from jax.experimental.pallas import tpu as pltpu

F32 = jnp.float32
BF16 = jnp.bfloat16

D_MODEL = 1024
D_FF = 2816
DEPTH = 4
CHUNK = 64
DN_HEADS = 4
DN_HEAD_DIM = 128
DN_DIM = DN_HEADS * DN_HEAD_DIM
CONV_WIDTH = 4
CONV_DIM = 3 * DN_DIM
SWA_HEADS = 8
SWA_KV_HEADS = 2
SWA_GROUP = SWA_HEADS // SWA_KV_HEADS
SWA_HEAD_DIM = 64
SWA_DIM = SWA_HEADS * SWA_HEAD_DIM
SWA_KV_DIM = SWA_KV_HEADS * SWA_HEAD_DIM
WINDOW = 128
WINDOW_CHUNKS = WINDOW // CHUNK
RMS_EPS = 1e-6
L2_EPS = 1e-6
NEG_INF = -1e30
IN_SPLITS = (CONV_DIM, DN_DIM, DN_HEADS, DN_HEADS, SWA_DIM, SWA_KV_DIM, SWA_KV_DIM, D_MODEL, D_MODEL)

LANES = 128
SUBLANES = 8
SOLVE_BLOCK = 16
FF_SPLIT = 1536
VMEM_LIMIT = 56 * 1024 * 1024


def _dot(a, b):
    return jnp.dot(a, b, preferred_element_type=F32)


def _dot_nt(a, b):
    return lax.dot_general(a, b, (((1,), (1,)), ((), ())), preferred_element_type=F32)


def _dot_exact(a, b):
    return jnp.dot(a, b, preferred_element_type=F32, precision=lax.Precision.HIGHEST)


def _rms(x, w):
    return x * lax.rsqrt(jnp.mean(x * x, axis=-1, keepdims=True) + RMS_EPS) * w


def _sigmoid(x):
    return 1.0 / (1.0 + jnp.exp(-x))


def _silu_tanh(x):
    hx = 0.5 * x
    return hx + hx * jnp.tanh(hx)


def _softplus(x):
    return jnp.maximum(x, 0.0) + jnp.log(1.0 + jnp.exp(-jnp.abs(x)))


def _resident(shape, layer):
    zeros = (0,) * len(shape)
    return pl.BlockSpec((None,) + tuple(shape), lambda *_: (layer,) + zeros,
                        pipeline_mode=pl.Buffered(1))


def _params(n_axes):
    return pltpu.CompilerParams(dimension_semantics=("arbitrary",) * n_axes,
                                vmem_limit_bytes=VMEM_LIMIT)


def _ffn_residual(x, nw_ref, wg_ref, wu_ref, wd_ref):
    hb = _rms(x, nw_ref[...]).astype(BF16)
    acc = None
    for lo, hi in ((0, FF_SPLIT), (FF_SPLIT, D_FF)):
        g = _dot(hb, wg_ref[:, lo:hi])
        u = _dot(hb, wu_ref[:, lo:hi])
        a = (g * _sigmoid(g) * u).astype(BF16)
        y = _dot(a, wd_ref[lo:hi, :])
        acc = y if acc is None else acc + y
    return x + 0.5 * acc


def _ffn_kernel(x_ref, nw_ref, wg_ref, wu_ref, wd_ref, o_ref):
    o_ref[...] = _ffn_residual(x_ref[...], nw_ref, wg_ref, wu_ref, wd_ref)


def _ffn(x, nw, wg, wu, wd, layer, tm):
    n = x.shape[0]
    row = pl.BlockSpec((tm, D_MODEL), lambda i: (i, 0))
    return pl.pallas_call(
        _ffn_kernel,
        grid=(n // tm,),
        in_specs=[row, _resident((1, D_MODEL), layer), _resident((D_MODEL, D_FF), layer),
                  _resident((D_MODEL, D_FF), layer), _resident((D_FF, D_MODEL), layer)],
        out_specs=row,
        out_shape=jax.ShapeDtypeStruct((n, D_MODEL), F32),
        compiler_params=_params(1),
        name="ffn",
    )(x, nw, wg, wu, wd)


_IN_GROUPS = (CONV_DIM, DN_DIM, SWA_DIM, SWA_KV_DIM, SWA_KV_DIM, D_MODEL, D_MODEL)
_IN_MAIN = sum(_IN_GROUPS)
_IN_PIECE = 512


def _conv_silu(x, prev, cw):
    sub = lax.broadcasted_iota(jnp.int32, (SUBLANES, x.shape[1]), 0)
    y = x * cw[CONV_WIDTH - 1:CONV_WIDTH]
    for sh in range(1, CONV_WIDTH):
        rolled = pltpu.roll(x, sh, 0)
        head = jnp.where(sub < sh, pltpu.roll(prev, sh, 0), rolled[0:SUBLANES])
        shifted = jnp.concatenate([head, rolled[SUBLANES:]], axis=0)
        y = y + shifted * cw[CONV_WIDTH - 1 - sh:CONV_WIDTH - sh]
    return _silu_tanh(y)


def _l2_heads(y):
    q_n, k_n, v_n = [], [], []
    for h in range(DN_HEADS):
        q = y[:, h * DN_HEAD_DIM:(h + 1) * DN_HEAD_DIM]
        k = y[:, DN_DIM + h * DN_HEAD_DIM:DN_DIM + (h + 1) * DN_HEAD_DIM]
        q_n.append(q * (lax.rsqrt(jnp.sum(q * q, axis=-1, keepdims=True) + L2_EPS) * (DN_HEAD_DIM ** -0.5)))
        k_n.append(k * lax.rsqrt(jnp.sum(k * k, axis=-1, keepdims=True) + L2_EPS))
        v_n.append(y[:, 2 * DN_DIM + h * DN_HEAD_DIM:2 * DN_DIM + (h + 1) * DN_HEAD_DIM])
    return q_n, k_n, v_n


def _inproj_kernel(x_ref, nw_ref, w_ref, wba_ref, *rest, tm, tiles_per_seq):
    fused = tiles_per_seq > 0
    if fused:
        (convw_ref, cbuf0_ref, qn_ref, kn_ref, vv_ref, ctail_ref,
         gate_ref, qs_ref, ks_ref, vs_ref, ga_ref, gb_ref, ba_ref, bat_ref, c8_ref) = rest
    else:
        qkv_ref, gate_ref, qs_ref, ks_ref, vs_ref, ga_ref, gb_ref, ba_ref, bat_ref = rest
    if fused:
        @pl.when(pl.program_id(0) == 0)
        def _():
            c8_ref[...] = jnp.zeros((SUBLANES, CONV_DIM), F32)

    hb = _rms(x_ref[...], nw_ref[...]).astype(BF16)
    z = _dot(hb, w_ref[:, 0:CONV_DIM])
    if fused:
        init8 = jnp.concatenate([jnp.zeros((SUBLANES - (CONV_WIDTH - 1), CONV_DIM), F32), cbuf0_ref[...]], axis=0)
        prev8 = jnp.where(pl.program_id(0) % tiles_per_seq == 0, init8, c8_ref[...])
        c8_ref[...] = z[tm - SUBLANES:tm]
        ctail_ref[...] = z[tm - (CONV_WIDTH - 1):tm]
        cw = convw_ref[...]
    else:
        qkv_ref[...] = z
    pieces = []
    off = CONV_DIM
    for ref, width in zip((gate_ref, qs_ref, ks_ref, vs_ref, ga_ref, gb_ref), _IN_GROUPS[1:]):
        for lo in range(0, width, _IN_PIECE):
            pieces.append((width, ref, lo, min(lo + _IN_PIECE, width), off))
        off += width
    pieces.sort(key=lambda t: -t[0])
    slab = tm // len(pieces) if tm % (len(pieces) * 2 * SUBLANES) == 0 else tm
    for g, (_, ref, lo, hi, off) in enumerate(pieces):
        ref[:, lo:hi] = _dot(hb, w_ref[:, off + lo:off + hi]).astype(ref.dtype)
        if fused and g * slab < tm:
            rows = slice(g * slab, (g + 1) * slab)
            hist = prev8 if g == 0 else z[g * slab - SUBLANES:g * slab]
            for strip in range(CONV_DIM // DN_HEAD_DIM):
                cols = slice(strip * DN_HEAD_DIM, (strip + 1) * DN_HEAD_DIM)
                y = _conv_silu(z[rows, cols], hist[:, cols], cw[:, cols])
                kind, h = divmod(strip, DN_HEADS)
                sl = slice(h * DN_HEAD_DIM, (h + 1) * DN_HEAD_DIM)
                if kind == 0:
                    y = y * (lax.rsqrt(jnp.sum(y * y, axis=-1, keepdims=True) + L2_EPS) * (DN_HEAD_DIM ** -0.5))
                    qn_ref[rows, sl] = y.astype(BF16)
                elif kind == 1:
                    y = y * lax.rsqrt(jnp.sum(y * y, axis=-1, keepdims=True) + L2_EPS)
                    kn_ref[rows, sl] = y.astype(BF16)
                else:
                    vv_ref[rows, sl] = y.astype(BF16)
    ba = _dot(hb, wba_ref[...])
    ba_ref[...] = ba
    bat_ref[...] = ba.T[0:8]


def _inproj(x, nw, w_main, w_ba, conv_w, cbuf0, layer, tm, seq, fused):
    n = x.shape[0]
    row = lambda width: pl.BlockSpec((tm, width), lambda i: (i, 0))
    tail_shapes = [jax.ShapeDtypeStruct((n, w), dt)
                   for w, dt in zip(_IN_GROUPS[1:], (BF16, BF16, F32, F32, BF16, BF16))]
    tail_shapes += [jax.ShapeDtypeStruct((n, LANES), F32), jax.ShapeDtypeStruct((8, n), F32)]
    tail_specs = [row(w) for w in _IN_GROUPS[1:]] + [row(LANES), pl.BlockSpec((8, tm), lambda i: (0, i))]
    in_specs = [row(D_MODEL), _resident((1, D_MODEL), layer), _resident((D_MODEL, _IN_MAIN), layer),
                _resident((D_MODEL, LANES), layer)]
    args = [x, nw, w_main, w_ba]
    if fused:
        assert seq % tm == 0
        tps = seq // tm
        per_seq = pl.BlockSpec((None, CONV_WIDTH - 1, CONV_DIM), lambda i: (i // tps, 0, 0))
        in_specs += [_resident((CONV_WIDTH, CONV_DIM), layer), per_seq]
        args += [conv_w, cbuf0]
        head_shapes = [jax.ShapeDtypeStruct((n, DN_DIM), BF16)] * 3
        head_shapes += [jax.ShapeDtypeStruct((n // seq, CONV_WIDTH - 1, CONV_DIM), F32)]
        head_specs = [row(DN_DIM)] * 3 + [per_seq]
        scratch = [pltpu.VMEM((SUBLANES, CONV_DIM), F32)]
    else:
        tps = 0
        head_shapes = [jax.ShapeDtypeStruct((n, CONV_DIM), F32)]
        head_specs = [row(CONV_DIM)]
        scratch = []
    outs = pl.pallas_call(
        functools.partial(_inproj_kernel, tm=tm, tiles_per_seq=tps),
        grid=(n // tm,),
        in_specs=in_specs,
        out_specs=head_specs + tail_specs,
        out_shape=head_shapes + tail_shapes,
        scratch_shapes=scratch,
        compiler_params=_params(1),
        name="inproj",
    )(*args)
    nh = len(head_shapes)
    return (tuple(outs[:nh]),) + tuple(outs[nh:])


def _unit_lower_inverses(a_diags, a_offs, eye):
    bf = lambda x: x.astype(BF16)
    n = eye.shape[0]
    ps = [bf(a) for a in a_diags]
    tds = [eye - a for a in a_diags]
    ps = [bf(_dot(p, p)) for p in ps]
    n_sq = SOLVE_BLOCK.bit_length() - 2
    for _ in range(n_sq - 1):
        both = [_dot(jnp.concatenate([bf(td), p], axis=0), p) for td, p in zip(tds, ps)]
        tds = [td + r[:n] for td, r in zip(tds, both)]
        ps = [bf(r[n:]) for r in both]
    tds = [td + _dot(bf(td), p) for td, p in zip(tds, ps)]
    tdbs = [bf(td) for td in tds]
    bs = [_dot(tdb, bf(a)) for tdb, a in zip(tdbs, a_offs)]
    bbs = [bf(b) for b in bs]
    b2s = [_dot(bb, bb) for bb in bbs]
    xs = [bf(_dot(bf(eye + b2), tdb)) for b2, tdb in zip(b2s, tdbs)]
    return [_dot(bf(eye - b), x) for b, x in zip(bs, xs)]


def _dn_kernel(*refs, tb, sb, chunk, pre):
    if pre:
        (qn_ref, kn_ref, vv_ref, gate_ref, ba_ref, bat_ref, lanep_ref, rowp_ref, dnw_ref, s0_ref,
         o_ref, sout_ref, s_ref) = refs
    else:
        (qkv_ref, gate_ref, ba_ref, bat_ref, lanep_ref, rowp_ref, dnw_ref, s0_ref, convw_ref, cbuf0_ref,
         o_ref, sout_ref, cout_ref, s_ref, c8_ref) = refs
    i = pl.program_id(1)
    last = pl.num_programs(1) - 1
    nc = tb // chunk
    cps = sb // chunk

    @pl.when(i == 0)
    def _():
        s_ref[...] = s0_ref[...]

    if pre:
        hsl = lambda h: slice(h * DN_HEAD_DIM, (h + 1) * DN_HEAD_DIM)
        q_n = [qn_ref[:, hsl(h)].astype(F32) for h in range(DN_HEADS)]
        k_n = [kn_ref[:, hsl(h)].astype(F32) for h in range(DN_HEADS)]
        v_n = [vv_ref[:, hsl(h)].astype(F32) for h in range(DN_HEADS)]
    else:
        @pl.when(i == 0)
        def _():
            c8_ref[...] = jnp.zeros((SUBLANES, CONV_DIM), F32)
            c8_ref[SUBLANES - (CONV_WIDTH - 1):SUBLANES, :] = cbuf0_ref[...]

        x = qkv_ref[...]
        q_n, k_n, v_n = _l2_heads(_conv_silu(x, c8_ref[...], convw_ref[...]))
        c8_ref[...] = x[tb - SUBLANES:tb]

        @pl.when(i == last)
        def _():
            cout_ref[...] = x[tb - (CONV_WIDTH - 1):tb]

    ba = ba_ref[...]
    lanep = lanep_ref[...]
    beta_all = _sigmoid(ba)
    g_all = -jnp.exp(lanep[0:1]) * _softplus(ba + lanep[1:2])
    rowp = rowp_ref[...]
    g_rows = -jnp.exp(rowp[0]) * _softplus(bat_ref[...] + rowp[1])

    r = lax.broadcasted_iota(jnp.int32, (sb, sb), 0)
    c = lax.broadcasted_iota(jnp.int32, (sb, sb), 1)
    same_chunk = (r // chunk) == (c // chunk)
    incl = same_chunk & (c <= r)
    def split3(a, axis):
        rnd = lambda t: t.astype(BF16).astype(F32)
        hi = rnd(a)
        mid = rnd(a - hi)
        lo = rnd(a - hi - mid)
        return jnp.concatenate([hi, mid, lo], axis=axis).astype(BF16)

    tri_lower = jnp.where(incl, 1.0, 0.0).astype(BF16)
    tri_upper = jnp.where(same_chunk & (r <= c), 1.0, 0.0).astype(BF16)
    g_all3 = split3(g_all, 1)
    g_rows3 = split3(g_rows, 0)
    col_parts, row_parts = [], []
    for s in range(tb // sb):
        cs = _dot(tri_lower, g_all3[s * sb:(s + 1) * sb])
        col_parts.append(cs[:, :LANES] + cs[:, LANES:2 * LANES] + cs[:, 2 * LANES:])
        rsum = _dot(g_rows3[:, s * sb:(s + 1) * sb], tri_upper)
        row_parts.append(rsum[0:8] + rsum[8:16] + rsum[16:24])
    gc_all = jnp.concatenate(col_parts, axis=0) if len(col_parts) > 1 else col_parts[0]
    gc_rows = jnp.concatenate(row_parts, axis=1) if len(row_parts) > 1 else row_parts[0]

    same_blk = (r // SOLVE_BLOCK) == (c // SOLVE_BLOCK)
    strict_diag = same_blk & (c < r)
    strict_off = same_chunk & (c < r) & jnp.logical_not(same_blk)
    eye = jnp.where(r == c, 1.0, 0.0).astype(F32)

    probs = [(h, s) for h in range(DN_HEADS) for s in range(tb // sb)]
    kb_n, gc_n, rhs_n, qd_n = [], [], [], []
    for h in range(DN_HEADS):
        beta_h = beta_all[:, h:h + 1]
        gc_h = gc_all[:, DN_HEADS + h:DN_HEADS + h + 1]
        egc_h = jnp.exp(gc_h)
        kb_h = k_n[h] * beta_h
        kb_n.append(kb_h)
        gc_n.append(gc_h)
        rhs_n.append(jnp.concatenate([kb_h * egc_h, v_n[h] * beta_h], axis=1).astype(BF16))
        qd_n.append(q_n[h] * egc_h)
    rs = lambda s: slice(s * sb, (s + 1) * sb)
    kts = [k_n[h][rs(s)].T.astype(BF16) for h, s in probs]
    decays = [jnp.exp(jnp.where(incl, gc_n[h][rs(s)] - gc_rows[DN_HEADS + h:DN_HEADS + h + 1, rs(s)], NEG_INF))
              for h, s in probs]
    kq = [_dot(jnp.concatenate([kb_n[h][rs(s)], q_n[h][rs(s)]], axis=0).astype(BF16), kt)
          for (h, s), kt in zip(probs, kts)]
    akds = [r[:sb] * d for r, d in zip(kq, decays)]
    intras = [r[sb:] * d for r, d in zip(kq, decays)]
    t_mats = _unit_lower_inverses([jnp.where(strict_diag, a, 0.0) for a in akds],
                                  [jnp.where(strict_off, a, 0.0) for a in akds], eye)
    wus = [_dot(t.astype(BF16), rhs_n[h][rs(s)]) for (h, s), t in zip(probs, t_mats)]
    keys, lhs, wub, qds = [], [], [], []
    for (h, s), wu, intra in zip(probs, wus, intras):
        k = k_n[h][rs(s)]
        gc = gc_n[h][rs(s)]
        for cc in range(cps):
            rr = slice(cc * chunk, (cc + 1) * chunk)
            g_last = gc[(cc + 1) * chunk - 1:(cc + 1) * chunk, :]
            krt = (k[rr] * jnp.exp(g_last - gc[rr])).T
            keys.append((h, s * cps + cc))
            lhs.append(jnp.concatenate([krt, intra[rr, rr]], axis=0).astype(BF16))
            wub.append(wu[rr].astype(BF16))
            qds.append(qd_n[h][rs(s)][rr])
    pre = [_dot(a, b) for a, b in zip(lhs, wub)]
    hd = DN_HEAD_DIM
    gq_v = {key: jnp.concatenate([r[:hd, :hd], qd - r[hd:, :hd]], axis=0).astype(BF16)
            for key, r, qd in zip(keys, pre, qds)}
    nu_v = {key: r[:hd, hd:] for key, r in zip(keys, pre)}
    o0_v = {key: r[hd:, hd:] for key, r in zip(keys, pre)}

    heads = range(DN_HEADS)
    state = [s_ref[h] for h in heads]
    o_rows = [[] for _ in heads]
    for j in range(nc):
        res = [_dot(gq_v[h, j], state[h].astype(BF16)) for h in heads]
        for h in heads:
            g_end = gc_all[(j + 1) * chunk - 1:(j + 1) * chunk, DN_HEADS + h:DN_HEADS + h + 1]
            state[h] = state[h] * jnp.exp(g_end) - res[h][:hd] + nu_v[h, j]
        for h in heads:
            o_rows[h].append(res[h][hd:] + o0_v[h, j])

    dnw = dnw_ref[...]
    for h in heads:
        sl = slice(h * DN_HEAD_DIM, (h + 1) * DN_HEAD_DIM)
        s_ref[h] = state[h]
        o_h = jnp.concatenate(o_rows[h], axis=0) if nc > 1 else o_rows[h][0]
        o_ref[:, sl] = (_rms(o_h, dnw) * _silu_tanh(gate_ref[:, sl].astype(F32))).astype(o_ref.dtype)

    @pl.when(i == last)
    def _():
        sout_ref[...] = s_ref[...]


def _deltanet(dn_in, gate, ba, bat, conv_w, lanep, rowp, dn_w, cbuf0, s0, layer, batch, seq, tb, sb, chunk):
    pre = len(dn_in) > 1
    nb = seq // tb
    nc = tb // chunk
    row = lambda width: pl.BlockSpec((tb, width), lambda b, i: (b * nb + i, 0))
    per_batch = lambda shape: pl.BlockSpec((None,) + shape, lambda b, i: (b,) + (0,) * len(shape))
    state = (DN_HEADS, DN_HEAD_DIM, DN_HEAD_DIM)
    common_specs = [row(DN_DIM), row(LANES), pl.BlockSpec((None, 8, tb), lambda b, i: (b * nb + i, 0, 0)),
                    _resident((2, LANES), layer), _resident((2, 8, 1), layer), _resident((1, DN_HEAD_DIM), layer),
                    per_batch(state)]
    common_args = [gate, ba, bat.reshape(8, batch * nb, tb).swapaxes(0, 1), lanep, rowp, dn_w, s0]
    out_specs = [row(DN_DIM), per_batch(state)]
    out_shape = [jax.ShapeDtypeStruct((batch * seq, DN_DIM), BF16), jax.ShapeDtypeStruct((batch,) + state, F32)]
    scratch = [pltpu.VMEM(state, F32)]
    if pre:
        in_specs = [row(DN_DIM)] * 3 + common_specs
        args = list(dn_in[:3]) + common_args
    else:
        conv_rows = (CONV_WIDTH - 1, CONV_DIM)
        in_specs = [row(CONV_DIM)] + common_specs + [_resident((CONV_WIDTH, CONV_DIM), layer), per_batch(conv_rows)]
        args = [dn_in[0]] + common_args + [conv_w, cbuf0]
        out_specs.append(per_batch(conv_rows))
        out_shape.append(jax.ShapeDtypeStruct((batch,) + conv_rows, F32))
        scratch.append(pltpu.VMEM((SUBLANES, CONV_DIM), F32))
    outs = pl.pallas_call(
        functools.partial(_dn_kernel, tb=tb, sb=sb, chunk=chunk, pre=pre),
        grid=(batch, nb),
        in_specs=in_specs,
        out_specs=out_specs,
        out_shape=out_shape,
        scratch_shapes=scratch,
        compiler_params=_params(2),
        name="deltanet",
    )(*args)
    return outs[0], outs[1], (dn_in[3] if pre else outs[2])


def _half_rms(x, w, lo):
    x2 = x * x
    s_lo = jnp.sum(jnp.where(lo, x2, 0.0), axis=-1, keepdims=True)
    s_hi = jnp.sum(jnp.where(lo, 0.0, x2), axis=-1, keepdims=True)
    inv = 1.0 / SWA_HEAD_DIM
    scale = jnp.where(lo, lax.rsqrt(s_lo * inv + RMS_EPS), lax.rsqrt(s_hi * inv + RMS_EPS))
    return x * scale * w


def _swa_kernel(sinks_ref, q_ref, k_ref, v_ref, qw_ref, kw_ref, k0_ref, v0_ref,
                o_ref, kn_ref, kbuf, vbuf, bias_s, *, tq, gq, layer, first_pos):
    i = pl.program_id(1)
    kw_len = WINDOW + gq
    combos = [(kv, half) for kv in range(SWA_KV_HEADS) for half in range(2)]

    @pl.when(i == 0)
    def _():
        kbuf[0:WINDOW, :] = k0_ref[...]
        vbuf[0:WINDOW, :] = v0_ref[...]

    qi = lax.broadcasted_iota(jnp.int32, (2 * gq, kw_len), 0)
    kj = lax.broadcasted_iota(jnp.int32, (2 * gq, kw_len), 1)
    upper = qi >= gq

    @pl.when((pl.program_id(0) == 0) & (i == 0))
    def _():
        ri = jnp.where(upper, qi - gq, qi)
        dist = jnp.abs(WINDOW + ri - kj).astype(F32)
        qc = (ri + WINDOW) // CHUNK
        kc = kj // CHUNK
        vis = (kc <= qc) & (kc >= qc - WINDOW_CHUNKS)
        for n, (kv, half) in enumerate(combos):
            h_a = SWA_GROUP * kv + half
            slope = jnp.where(upper, 2.0 ** (-8.0 * (h_a + 3) / SWA_HEADS), 2.0 ** (-8.0 * (h_a + 1) / SWA_HEADS))
            bias_s[n] = jnp.where(vis, -slope * dist, NEG_INF)

    lo = lax.broadcasted_iota(jnp.int32, (1, LANES), 1) < SWA_HEAD_DIM
    kn = _half_rms(k_ref[...], kw_ref[...], lo)
    kn_ref[...] = kn
    kbuf[WINDOW:WINDOW + tq, :] = kn
    vbuf[WINDOW:WINDOW + tq, :] = v_ref[...]

    def variants(x):
        sw = pltpu.roll(x, SWA_HEAD_DIM, 1)
        a_lo = jnp.where(lo, x, 0.0).astype(BF16)
        a_hi = jnp.where(lo, 0.0, x).astype(BF16)
        b_lo = jnp.where(lo, sw, 0.0).astype(BF16)
        b_hi = jnp.where(lo, 0.0, sw).astype(BF16)
        return {(0, 0): a_lo, (0, 1): b_hi, (1, 0): b_lo, (1, 1): a_hi}

    kvar = variants(kbuf[...])
    vvar = variants(vbuf[...])

    qw = qw_ref[...]
    qn = [(_half_rms(q_ref[:, t * LANES:(t + 1) * LANES].astype(F32), qw[:, t * LANES:(t + 1) * LANES], lo)
           * (SWA_HEAD_DIM ** -0.5)).astype(BF16) for t in range(SWA_HEADS // 2)]

    for g in range(tq // gq):
        rows = slice(g * gq, (g + 1) * gq)
        win = slice(g * gq, g * gq + kw_len)
        for kv in range(SWA_KV_HEADS):
            q2 = jnp.concatenate([qn[2 * kv][rows], qn[2 * kv + 1][rows]], axis=0)
            acc = None
            for half in range(2):
                n = 2 * kv + half
                h_a = SWA_GROUP * kv + half
                s = _dot_nt(q2, kvar[kv, half][win]) + bias_s[n]
                if first_pos < WINDOW and g * gq < WINDOW:
                    s = jnp.where(kj >= WINDOW - first_pos - i * tq - g * gq, s, NEG_INF)
                sink = jnp.where(upper[:, 0:1], sinks_ref[layer, h_a + 2], sinks_ref[layer, h_a])
                m = jnp.maximum(jnp.max(s, axis=-1, keepdims=True), sink)
                p = jnp.exp(s - m)
                denom = jnp.sum(p, axis=-1, keepdims=True) + jnp.exp(sink - m)
                o = _dot(p.astype(BF16), vvar[kv, half][win]) / denom
                acc = o if acc is None else acc + o
            o_ref[rows, 2 * kv * LANES:(2 * kv + 1) * LANES] = acc[:gq].astype(o_ref.dtype)
            o_ref[rows, (2 * kv + 1) * LANES:(2 * kv + 2) * LANES] = acc[gq:].astype(o_ref.dtype)

    if tq >= WINDOW:
        knext = kbuf[tq:tq + WINDOW, :]
        vnext = vbuf[tq:tq + WINDOW, :]
        kbuf[0:WINDOW, :] = knext
        vbuf[0:WINDOW, :] = vnext


def _swa(sinks, q, k, v, qw, kw, k0, v0, layer, batch, seq, tq, gq, first_pos):
    nb = seq // tq
    assert tq >= WINDOW or nb == 1
    assert gq % CHUNK == 0 or (gq == seq and nb == 1)
    assert first_pos % CHUNK == 0
    row = lambda width: pl.BlockSpec((tq, width), lambda b, i: (b * nb + i, 0))
    per_batch = pl.BlockSpec((None, WINDOW, SWA_KV_DIM), lambda b, i: (b, 0, 0))
    kern = functools.partial(_swa_kernel, tq=tq, gq=gq, layer=layer, first_pos=first_pos)
    return pl.pallas_call(
        kern,
        grid=(batch, nb),
        in_specs=[pl.BlockSpec(memory_space=pltpu.SMEM),
                  row(SWA_DIM), row(SWA_KV_DIM), row(SWA_KV_DIM),
                  _resident((1, SWA_DIM), layer), _resident((1, SWA_KV_DIM), layer),
                  per_batch, per_batch],
        out_specs=[row(SWA_DIM), row(SWA_KV_DIM)],
        out_shape=[jax.ShapeDtypeStruct((batch * seq, SWA_DIM), BF16),
                   jax.ShapeDtypeStruct((batch * seq, SWA_KV_DIM), F32)],
        scratch_shapes=[pltpu.VMEM((WINDOW + tq, SWA_KV_DIM), F32),
                        pltpu.VMEM((WINDOW + tq, SWA_KV_DIM), F32),
                        pltpu.VMEM((2 * SWA_KV_HEADS, 2 * gq, WINDOW + gq), F32)],
        compiler_params=_params(2),
        name="swa",
    )(sinks, q, k, v, qw, kw, k0, v0)


def _merge_ffn_kernel(x_ref, odn_ref, oswa_ref, ga_ref, gb_ref, wdn_ref, wswa_ref, wout_ref,
                      nw_ref, wg_ref, wu_ref, wd_ref, o_ref):
    y_dn = _dot(odn_ref[...], wdn_ref[...])
    y_swa = _dot(oswa_ref[...], wswa_ref[...])
    merged = _sigmoid(ga_ref[...].astype(F32)) * y_dn + _sigmoid(gb_ref[...].astype(F32)) * y_swa
    x_mid = x_ref[...] + _dot(merged.astype(BF16), wout_ref[...])
    o_ref[...] = _ffn_residual(x_mid, nw_ref, wg_ref, wu_ref, wd_ref)


def _merge_ffn(x, odn, oswa, ga, gb, wdn, wswa, wout, nw, wg, wu, wd, layer, tm):
    n = x.shape[0]
    row = lambda width: pl.BlockSpec((tm, width), lambda i: (i, 0))
    return pl.pallas_call(
        _merge_ffn_kernel,
        grid=(n // tm,),
        in_specs=[row(D_MODEL), row(DN_DIM), row(SWA_DIM), row(D_MODEL), row(D_MODEL),
                  _resident((DN_DIM, D_MODEL), layer), _resident((SWA_DIM, D_MODEL), layer),
                  _resident((D_MODEL, D_MODEL), layer),
                  _resident((1, D_MODEL), layer), _resident((D_MODEL, D_FF), layer),
                  _resident((D_MODEL, D_FF), layer), _resident((D_FF, D_MODEL), layer)],
        out_specs=row(D_MODEL),
        out_shape=jax.ShapeDtypeStruct((n, D_MODEL), F32),
        compiler_params=_params(1),
        name="merge_ffn",
    )(x, odn, oswa, ga, gb, wdn, wswa, wout, nw, wg, wu, wd)


def _prepare(ffn1_norm, ffn1_wg, ffn1_wu, ffn1_wd, mix_norm, w_in, conv_w, a_log, dt_bias, dn_norm,
             q_norm, k_norm, sinks, w_o_dn, w_o_swa, w_out, ffn2_norm, ffn2_wg, ffn2_wu, ffn2_wd):
    depth = w_in.shape[0]
    offs = [0]
    for s in IN_SPLITS:
        offs.append(offs[-1] + s)
    col = lambda idx: w_in[:, :, offs[idx]:offs[idx + 1]]
    w_main = jnp.concatenate([col(0), col(1), col(4), col(5), col(6), col(7), col(8)], axis=2).astype(BF16)
    w_ba8 = jnp.concatenate([col(2), col(3)], axis=2)
    w_ba = jnp.pad(w_ba8, ((0, 0), (0, 0), (0, LANES - 2 * DN_HEADS))).astype(BF16)
    zeros4 = jnp.zeros((depth, DN_HEADS), F32)
    lane_pad = lambda a: jnp.pad(jnp.concatenate([zeros4, a.astype(F32)], axis=1),
                                 ((0, 0), (0, LANES - 2 * DN_HEADS)))
    lanep = jnp.stack([lane_pad(a_log), lane_pad(dt_bias)], axis=1)
    row_pad = lambda a: jnp.concatenate([zeros4, a.astype(F32)], axis=1)[:, :, None]
    rowp = jnp.stack([row_pad(a_log), row_pad(dt_bias)], axis=1)
    r3 = lambda a: a.astype(F32)[:, None, :]
    return dict(
        ffn1=(r3(ffn1_norm), ffn1_wg.astype(BF16), ffn1_wu.astype(BF16), ffn1_wd.astype(BF16)),
        ffn2=(r3(ffn2_norm), ffn2_wg.astype(BF16), ffn2_wu.astype(BF16), ffn2_wd.astype(BF16)),
        mix_norm=r3(mix_norm), w_main=w_main, w_ba=w_ba,
        conv_w=conv_w.astype(F32), lanep=lanep, rowp=rowp, dn_norm=r3(dn_norm),
        q_norm=jnp.tile(q_norm.astype(F32), (1, SWA_HEADS))[:, None, :],
        k_norm=jnp.tile(k_norm.astype(F32), (1, SWA_KV_HEADS))[:, None, :],
        sinks=sinks.astype(F32),
        w_o_dn=w_o_dn.astype(BF16), w_o_swa=w_o_swa.astype(BF16), w_out=w_out.astype(BF16),
    )


def _trunk(x, first_pos, conv_state, dn_state, k_cache, v_cache, p, *, tm, tb, sb, tq, gq):
    batch, seq, _ = x.shape
    depth = p["w_main"].shape[0]
    chunk = CHUNK if seq >= CHUNK else seq
    x = x.reshape(batch * seq, D_MODEL)
    dn_out, conv_out, k_out, v_out = [], [], [], []
    for l in range(depth):
        x = _ffn(x, *p["ffn1"], l, tm)
        if conv_state is None:
            cbuf0 = jnp.zeros((batch, CONV_WIDTH - 1, CONV_DIM), F32)
            s0 = jnp.zeros((batch, DN_HEADS, DN_HEAD_DIM, DN_HEAD_DIM), F32)
            k0 = jnp.zeros((batch, WINDOW, SWA_KV_DIM), F32)
            v0 = k0
        else:
            cbuf0, s0 = conv_state[l], dn_state[l]
            k0 = k_cache[l].reshape(batch, WINDOW, SWA_KV_DIM)
            v0 = v_cache[l].reshape(batch, WINDOW, SWA_KV_DIM)
        dn_in, gate, qs, ks, vs, ga, gb, ba, bat = _inproj(
            x, p["mix_norm"], p["w_main"], p["w_ba"], p["conv_w"], cbuf0, l, tm, seq, seq % tm == 0)
        o_dn, s_new, conv_new = _deltanet(dn_in, gate, ba, bat, p["conv_w"], p["lanep"], p["rowp"], p["dn_norm"],
                                          cbuf0, s0, l, batch, seq, tb, sb, chunk)
        o_swa, kn = _swa(p["sinks"], qs, ks, vs, p["q_norm"], p["k_norm"], k0, v0, l, batch, seq, tq, gq, first_pos)
        x = _merge_ffn(x, o_dn, o_swa, ga, gb, p["w_o_dn"], p["w_o_swa"], p["w_out"], *p["ffn2"], l, tm)
        keep = min(WINDOW, seq)
        dn_out.append(s_new)
        conv_out.append(conv_new)
        kv_rows = lambda a: a.reshape(batch, seq, SWA_KV_DIM)[:, seq - keep:].reshape(
            batch, keep, SWA_KV_HEADS, SWA_HEAD_DIM)
        k_out.append(kv_rows(kn))
        v_out.append(kv_rows(vs))
    return (x.reshape(batch, seq, D_MODEL), jnp.stack(dn_out), jnp.stack(conv_out),
            jnp.stack(k_out), jnp.stack(v_out))


def kernel(x_prompt, x_sample, state_dn, state_conv, cache_swa_k, cache_swa_v, ffn1_norm, ffn1_wg, ffn1_wu, ffn1_wd, mix_norm, w_in, conv_w, a_log, dt_bias, dn_norm, q_norm, k_norm, sinks, w_o_dn, w_o_swa, w_out, ffn2_norm, ffn2_wg, ffn2_wu, ffn2_wd):
    p = _prepare(ffn1_norm, ffn1_wg, ffn1_wu, ffn1_wd, mix_norm, w_in, conv_w, a_log, dt_bias, dn_norm,
                 q_norm, k_norm, sinks, w_o_dn, w_o_swa, w_out, ffn2_norm, ffn2_wg, ffn2_wu, ffn2_wd)
    past_len = 4096
    assert cache_swa_k.shape[2] == WINDOW
    y_p, dn_p, conv_p, k_p, v_p = _trunk(x_prompt, 0, None, None, None, None, p,
                                         tm=512, tb=512, sb=128, tq=256, gq=128)
    dec_seq = x_sample.shape[1]
    y_s, dn_s, conv_s, k_s, v_s = _trunk(x_sample, past_len, state_conv, state_dn, cache_swa_k, cache_swa_v, p,
                                         tm=x_sample.shape[0] * dec_seq, tb=dec_seq, sb=dec_seq, tq=dec_seq, gq=dec_seq)
    return (y_p, y_s, dn_p, dn_s, conv_p, conv_s, k_p, v_p, k_s, v_s)
```
